```python
import math, functools
import jax, jax.numpy as jnp
from jax import lax
import numpy as np

D_MODEL = 2048
BATCH = 1
SEQ = 8192
DEPTH = 2
DEC_BATCH = 128
DEC_SEQ = 8
PAST_LEN = 16384
PAGE_SIZE = 128

N_BRANCH = 3
BRANCH_W = D_MODEL // 2
GLA_HEADS = 4
GLA_DV = BRANCH_W // GLA_HEADS
GLA_DK = GLA_DV // 2
GLA_GATE_RANK = 16
GLA_TAU = 16.0
GLA_CHUNK = 16
CONV_CH = BRANCH_W
CONV_WIDTH = 31
MLA_HEADS = 8
MLA_NOPE = BRANCH_W // MLA_HEADS
MLA_V = BRANCH_W // MLA_HEADS
MLA_ROPE = 64
MLA_Q_RANK = D_MODEL // 4
MLA_KV_RANK = D_MODEL // 8
MLA_SCALE = (MLA_NOPE + MLA_ROPE) ** -0.5
ROPE_THETA = 10000.0
Q_BLOCK = 128
D_FF = ((8 * D_MODEL // 3 + 255) // 256) * 256
N_EXPERTS = 8
TOP_K = 2
N_DENSE = (DEPTH + 1) // 2
N_MOE = DEPTH // 2
EPS = 1e-6
IN_SPLITS = (GLA_HEADS * GLA_DK, GLA_HEADS * GLA_DK, GLA_HEADS * GLA_DV, GLA_HEADS * GLA_DV,
             GLA_GATE_RANK, 2 * CONV_CH, MLA_Q_RANK, MLA_KV_RANK, MLA_ROPE, N_BRANCH * D_MODEL)
IN_DIM = sum(IN_SPLITS)

kernel_name = 'hybrid_gla_conformer_mla_decoder_step'


def rmsnorm(x, g):
    xf = x.astype(jnp.float32)
    y = xf * lax.rsqrt(jnp.mean(xf * xf, axis=-1, keepdims=True) + EPS)
    return (y * g.astype(jnp.float32)).astype(x.dtype)


def layernorm(x, g, b):
    xf = x.astype(jnp.float32)
    mu = jnp.mean(xf, axis=-1, keepdims=True)
    xc = xf - mu
    var = jnp.mean(xc * xc, axis=-1, keepdims=True)
    return (xc * lax.rsqrt(var + EPS) * g.astype(jnp.float32) + b.astype(jnp.float32)).astype(x.dtype)


def rope(x, pos):
    half = MLA_ROPE // 2
    inv = ROPE_THETA ** (-jnp.arange(half, dtype=jnp.float32) / half)
    ang = pos.astype(jnp.float32)[:, None] * inv[None, :]
    shape = (1, x.shape[1]) + (1,) * (x.ndim - 3) + (half,)
    cos = jnp.cos(ang).reshape(shape)
    sin = jnp.sin(ang).reshape(shape)
    xf = x.astype(jnp.float32)
    x1, x2 = xf[..., :half], xf[..., half:]
    return jnp.concatenate([x1 * cos - x2 * sin, x2 * cos + x1 * sin], axis=-1).astype(x.dtype)


def gla_recurrence(q, k, v, log_a, s0, chunk):
    B, T, H, DK = q.shape
    DV = v.shape[-1]
    n = T // chunk

    def to_chunks(z):
        return z.astype(jnp.float32).reshape(B, n, chunk, H, z.shape[-1]).transpose(1, 0, 3, 2, 4)

    causal = jnp.tril(jnp.ones((chunk, chunk), dtype=bool))

    def step(S, inp):
        qc, kc, vc, ac = inp
        b = jnp.cumsum(ac, axis=2)
        q_dec = qc * jnp.exp(b)
        k_inv = kc * jnp.exp(-b)
        att = jnp.where(causal, jnp.einsum('bhtk,bhsk->bhts', q_dec, k_inv), 0.0)
        o = jnp.einsum('bhtk,bhkv->bhtv', q_dec, S) + jnp.einsum('bhts,bhsv->bhtv', att, vc)
        b_end = b[:, :, -1:, :]
        S = (jnp.exp(b_end[:, :, 0, :, None]) * S
             + jnp.einsum('bhsk,bhsv->bhkv', kc * jnp.exp(b_end - b), vc))
        return S, o

    xs = (to_chunks(q), to_chunks(k), to_chunks(v), to_chunks(log_a))
    S, o = lax.scan(step, s0.astype(jnp.float32), xs)
    o = o.transpose(1, 0, 3, 2, 4).reshape(B, T, H, DV)
    return o.astype(v.dtype), S.astype(s0.dtype)


def conformer_conv(a, buf, w_dw, b_dw, ln_g, ln_b):
    xa = jnp.concatenate([buf.astype(a.dtype), a], axis=1)
    y = lax.conv_general_dilated(xa, w_dw[:, None, :].astype(a.dtype), (1,), 'VALID',
                                 dimension_numbers=('NWC', 'WIO', 'NWC'),
                                 feature_group_count=a.shape[-1]) + b_dw
    y = jax.nn.silu(layernorm(y, ln_g, ln_b))
    return y, xa[:, -(CONV_WIDTH - 1):]


def mla_core(qn, qp, q_pos, kn, k_pe, c_kv, k_pos, w_uv):
    s = jnp.einsum('bqhd,bkhd->bhqk', qn, kn) + jnp.einsum('bqhd,bkd->bhqk', qp, k_pe)
    s = s.astype(jnp.float32) * MLA_SCALE
    s = jnp.where(k_pos[None, None, None, :] <= q_pos[None, None, :, None], s, -jnp.inf)
    p = jax.nn.softmax(s, axis=-1).astype(c_kv.dtype)
    o_lat = jnp.einsum('bhqk,bkr->bqhr', p, c_kv)
    return jnp.einsum('bqhr,rhd->bqhd', o_lat, w_uv)


def attend_prompt(qn, qp, c_kv, k_pe, w_uk, g_kn, w_uv, pos):
    B, T = qn.shape[:2]
    kn = rmsnorm(jnp.einsum('btr,rhd->bthd', c_kv, w_uk), g_kn)
    nb = T // Q_BLOCK

    def blocks(z):
        return z.reshape((B, nb, Q_BLOCK) + z.shape[2:]).swapaxes(0, 1)

    def one_block(args):
        qn_b, qp_b, qpos_b = args
        return mla_core(qn_b, qp_b, qpos_b, kn, k_pe, c_kv, pos, w_uv)

    o = lax.map(one_block, (blocks(qn), blocks(qp), pos.reshape(nb, Q_BLOCK)))
    return o.swapaxes(0, 1).reshape(B, T, MLA_HEADS, MLA_V)


def attend_sample(qn, qp, c_kv, k_pe, w_uk, g_kn, w_uv, cache_ckv, cache_kpe, page_table, q_pos):
    T = qn.shape[1]
    past = page_table.shape[1] * cache_ckv.shape[1]
    k_pos = jnp.arange(past + T, dtype=jnp.int32)

    def one_seq(args):
        qn_i, qp_i, ckv_i, kpe_i, pt_i = args
        ckv_all = jnp.concatenate([cache_ckv[pt_i].reshape(past, MLA_KV_RANK), ckv_i], axis=0)[None]
        kpe_all = jnp.concatenate([cache_kpe[pt_i].reshape(past, MLA_ROPE), kpe_i], axis=0)[None]
        kn = rmsnorm(jnp.einsum('btr,rhd->bthd', ckv_all, w_uk), g_kn)
        return mla_core(qn_i[None], qp_i[None], q_pos, kn, kpe_all, ckv_all, k_pos, w_uv)[0]

    return lax.map(one_seq, (qn, qp, c_kv, k_pe, page_table))


def mixer_sublayer(x, pos, gla_s0, conv_buf, attend, mp):
    (g_mix, w_in, b_in, w_a2, b_a, g_gla, conv_w, conv_b, ln_g, ln_b,
     g_cq, g_ckv, g_kpe, w_uq, g_qn, g_qr, w_uk, g_kn, w_uv, w_branch, w_out) = mp
    B, T, _ = x.shape
    h = rmsnorm(x, g_mix)
    z = h @ w_in + b_in
    split_idx = np.cumsum(IN_SPLITS)[:-1].tolist()
    q_a, k_a, v_a, r_a, a_lr, glu_in, cq, ckv, kr, gate_logit = jnp.split(z, split_idx, axis=-1)

    qa = q_a.reshape(B, T, GLA_HEADS, GLA_DK) * GLA_DK ** -0.5
    ka = k_a.reshape(B, T, GLA_HEADS, GLA_DK)
    va = v_a.reshape(B, T, GLA_HEADS, GLA_DV)
    log_a = jax.nn.log_sigmoid((a_lr @ w_a2 + b_a).astype(jnp.float32)) / GLA_TAU
    o_a, s_a = gla_recurrence(qa, ka, va, log_a.reshape(B, T, GLA_HEADS, GLA_DK), gla_s0,
                              math.gcd(T, GLA_CHUNK))
    o_a = rmsnorm(o_a, g_gla).reshape(B, T, GLA_HEADS * GLA_DV) * jax.nn.silu(r_a)

    u, g = jnp.split(glu_in, 2, axis=-1)
    o_b, conv_new = conformer_conv(u * jax.nn.sigmoid(g), conv_buf, conv_w, conv_b, ln_g, ln_b)

    c_q = rmsnorm(cq, g_cq)
    q = jnp.einsum('btr,rhd->bthd', c_q, w_uq)
    qn = rmsnorm(q[..., :MLA_NOPE], g_qn)
    qp = rope(rmsnorm(q[..., MLA_NOPE:], g_qr), pos)
    c_kv = rmsnorm(ckv, g_ckv)
    k_pe = rope(rmsnorm(kr, g_kpe), pos)
    o_c = attend(qn, qp, c_kv, k_pe, w_uk, g_kn, w_uv).reshape(B, T, MLA_HEADS * MLA_V)

    branches = jnp.stack([o_a, o_b, o_c], axis=2)
    proj = jnp.einsum('btnw,nwd->btnd', branches, w_branch)
    gates = jax.nn.sigmoid(gate_logit.reshape(B, T, N_BRANCH, D_MODEL))
    y = jnp.sum(gates * proj, axis=2) @ w_out
    return x + y, c_kv, k_pe, s_a, conv_new


def swiglu(h, w_gate, w_up, w_down):
    return (jax.nn.silu(h @ w_gate) * (h @ w_up)) @ w_down


def dense_ffn(h, w_gate, w_up, w_down):
    return swiglu(h, w_gate, w_up, w_down)


def moe_ffn(h, w_router, w_gate, w_up, w_down):
    logits = (h @ w_router).astype(jnp.float32)
    top_v, top_i = lax.top_k(logits, TOP_K)
    top_w = jax.nn.softmax(top_v, axis=-1)
    gate = jnp.sum(jax.nn.one_hot(top_i, N_EXPERTS, dtype=jnp.float32) * top_w[..., None], axis=-2)
    out = jnp.zeros_like(h)
    for e in range(N_EXPERTS):
        out = out + gate[..., e:e + 1].astype(h.dtype) * swiglu(h, w_gate[e], w_up[e], w_down[e])
    return out


def setup_inputs(seed: int = 0) -> dict:
    key = jax.random.key(seed)
    ks = jax.random.split(key, 40)
    f32 = jnp.float32

    def nrm(k, shape, scale):
        return jax.random.normal(k, shape, f32) * scale

    def gain(k, shape):
        return 1.0 + 0.02 * jax.random.normal(k, shape, f32)

    D = D_MODEL
    n_pages = PAST_LEN // PAGE_SIZE
    n_used = DEC_BATCH * n_pages
    n_pool = n_used + max(1, n_used // 4)
    page_table = jax.random.permutation(ks[0], n_pool)[:n_used].reshape(DEC_BATCH, n_pages).astype(jnp.int32)
    return {
        'x_prompt': nrm(ks[1], (BATCH, SEQ, D), 1.0),
        'x_sample': nrm(ks[2], (DEC_BATCH, DEC_SEQ, D), 1.0),
        'cache_kv_latent': nrm(ks[3], (DEPTH, n_pool, PAGE_SIZE, MLA_KV_RANK), 1.0),
        'cache_k_rope': nrm(ks[4], (DEPTH, n_pool, PAGE_SIZE, MLA_ROPE), 1.0),
        'state_gla': nrm(ks[5], (DEPTH, DEC_BATCH, GLA_HEADS, GLA_DK, GLA_DV), 1.0),
        'state_conv': nrm(ks[6], (DEPTH, DEC_BATCH, CONV_WIDTH - 1, CONV_CH), 0.5),
        'page_table': page_table,
        'g_mix': gain(ks[7], (DEPTH, D)),
        'w_in': nrm(ks[8], (DEPTH, D, IN_DIM), D ** -0.5),
        'b_in': nrm(ks[9], (DEPTH, IN_DIM), 0.02),
        'gla_w_a2': nrm(ks[10], (DEPTH, GLA_GATE_RANK, GLA_HEADS * GLA_DK), GLA_GATE_RANK ** -0.5),
        'gla_b_a': nrm(ks[11], (DEPTH, GLA_HEADS * GLA_DK), 0.1),
        'gla_g_out': gain(ks[12], (DEPTH, GLA_HEADS, GLA_DV)),
        'conv_w': nrm(ks[13], (DEPTH, CONV_WIDTH, CONV_CH), CONV_WIDTH ** -0.5),
        'conv_b': nrm(ks[14], (DEPTH, CONV_CH), 0.02),
        'conv_ln_g': gain(ks[15], (DEPTH, CONV_CH)),
        'conv_ln_b': nrm(ks[16], (DEPTH, CONV_CH), 0.02),
        'mla_g_cq': gain(ks[17], (DEPTH, MLA_Q_RANK)),
        'mla_g_ckv': gain(ks[18], (DEPTH, MLA_KV_RANK)),
        'mla_g_kpe': gain(ks[19], (DEPTH, MLA_ROPE)),
        'mla_w_uq': nrm(ks[20], (DEPTH, MLA_Q_RANK, MLA_HEADS, MLA_NOPE + MLA_ROPE), MLA_Q_RANK ** -0.5),
        'mla_g_qn': gain(ks[21], (DEPTH, MLA_NOPE)),
        'mla_g_qr': gain(ks[22], (DEPTH, MLA_ROPE)),
        'mla_w_uk': nrm(ks[23], (DEPTH, MLA_KV_RANK, MLA_HEADS, MLA_NOPE), MLA_KV_RANK ** -0.5),
        'mla_g_kn': gain(ks[24], (DEPTH, MLA_NOPE)),
        'mla_w_uv': nrm(ks[25], (DEPTH, MLA_KV_RANK, MLA_HEADS, MLA_V), MLA_KV_RANK ** -0.5),
        'w_branch': nrm(ks[26], (DEPTH, N_BRANCH, BRANCH_W, D), BRANCH_W ** -0.5),
        'w_out': nrm(ks[27], (DEPTH, D, D), D ** -0.5),
        'g_ffn': gain(ks[28], (DEPTH, D)),
        'ffn_w_gate': nrm(ks[29], (N_DENSE, D, D_FF), D ** -0.5),
        'ffn_w_up': nrm(ks[30], (N_DENSE, D, D_FF), D ** -0.5),
        'ffn_w_down': nrm(ks[31], (N_DENSE, D_FF, D), D_FF ** -0.5),
        'moe_w_router': nrm(ks[32], (N_MOE, D, N_EXPERTS), D ** -0.5),
        'moe_w_gate': nrm(ks[33], (N_MOE, N_EXPERTS, D, D_FF), D ** -0.5),
        'moe_w_up': nrm(ks[34], (N_MOE, N_EXPERTS, D, D_FF), D ** -0.5),
        'moe_w_down': nrm(ks[35], (N_MOE, N_EXPERTS, D_FF, D), D_FF ** -0.5),
    }


def reference(x_prompt, x_sample, cache_kv_latent, cache_k_rope, state_gla, state_conv, page_table,
              g_mix, w_in, b_in, gla_w_a2, gla_b_a, gla_g_out, conv_w, conv_b, conv_ln_g, conv_ln_b,
              mla_g_cq, mla_g_ckv, mla_g_kpe, mla_w_uq, mla_g_qn, mla_g_qr, mla_w_uk, mla_g_kn, mla_w_uv,
              w_branch, w_out, g_ffn, ffn_w_gate, ffn_w_up, ffn_w_down,
              moe_w_router, moe_w_gate, moe_w_up, moe_w_down):
    b_p, t_p, _ = x_prompt.shape
    t_s = x_sample.shape[1]
    past = page_table.shape[1] * cache_kv_latent.shape[2]
    pos_p = jnp.arange(t_p, dtype=jnp.int32)
    pos_s = past + jnp.arange(t_s, dtype=jnp.int32)
    xp, xs = x_prompt, x_sample
    ckv_p, kpe_p, sg_p, sc_p = [], [], [], []
    ckv_s, kpe_s, sg_s, sc_s = [], [], [], []
    for l in range(DEPTH):
        mp = (g_mix[l], w_in[l], b_in[l], gla_w_a2[l], gla_b_a[l], gla_g_out[l],
              conv_w[l], conv_b[l], conv_ln_g[l], conv_ln_b[l],
              mla_g_cq[l], mla_g_ckv[l], mla_g_kpe[l], mla_w_uq[l], mla_g_qn[l], mla_g_qr[l],
              mla_w_uk[l], mla_g_kn[l], mla_w_uv[l], w_branch[l], w_out[l])
        xp, c, k, sg, sc = mixer_sublayer(
            xp, pos_p,
            jnp.zeros((b_p, GLA_HEADS, GLA_DK, GLA_DV), state_gla.dtype),
            jnp.zeros((b_p, CONV_WIDTH - 1, CONV_CH), xp.dtype),
            functools.partial(attend_prompt, pos=pos_p), mp)
        ckv_p.append(c); kpe_p.append(k); sg_p.append(sg); sc_p.append(sc)
        xs, c, k, sg, sc = mixer_sublayer(
            xs, pos_s, state_gla[l], state_conv[l],
            functools.partial(attend_sample, cache_ckv=cache_kv_latent[l], cache_kpe=cache_k_rope[l],
                              page_table=page_table, q_pos=pos_s), mp)
        ckv_s.append(c); kpe_s.append(k); sg_s.append(sg); sc_s.append(sc)
        if l % 2 == 0:
            ffn = functools.partial(dense_ffn, w_gate=ffn_w_gate[l // 2], w_up=ffn_w_up[l // 2],
                                    w_down=ffn_w_down[l // 2])
        else:
            ffn = functools.partial(moe_ffn, w_router=moe_w_router[l // 2], w_gate=moe_w_gate[l // 2],
                                    w_up=moe_w_up[l // 2], w_down=moe_w_down[l // 2])
        xp = xp + ffn(rmsnorm(xp, g_ffn[l]))
        xs = xs + ffn(rmsnorm(xs, g_ffn[l]))
    return (xp, xs,
            jnp.stack(ckv_p), jnp.stack(kpe_p), jnp.stack(sg_p), jnp.stack(sc_p),
            jnp.stack(ckv_s), jnp.stack(kpe_s), jnp.stack(sg_s), jnp.stack(sc_s))
```

```python
import functools
import math

import numpy as np
import jax
import jax.numpy as jnp
from jax import lax
from jax.experimental import pallas as pl
from jax.experimental.pallas import tpu as pltpu

F32 = jnp.float32
BF16 = jnp.bfloat16
EPS = 1e-6
HI = lax.Precision.HIGHEST

GLA_HEADS = 4
GLA_DK = 128
GLA_DV = 256
GLA_RANK = 16
GLA_TAU = 16.0
CONV_WIDTH = 31
HIST = CONV_WIDTH - 1
MLA_HEADS = 8
MLA_NOPE = 128
MLA_ROPE = 64
MLA_V = 128
MLA_SCALE = (MLA_NOPE + MLA_ROPE) ** -0.5
ROPE_THETA = 10000.0
N_EXPERTS = 8
LANE = 128
VMEM_LIMIT = 56 * 1024 * 1024


def _cp(sem, vmem=VMEM_LIMIT):
    return pltpu.CompilerParams(dimension_semantics=sem, vmem_limit_bytes=vmem)


def _sigmoid(x):
    return 1.0 / (1.0 + jnp.exp(-x))


def _log_sigmoid(x):
    return jnp.minimum(x, 0.0) - jnp.log(1.0 + jnp.exp(-jnp.abs(x)))


def _rms(x, g):
    return x * lax.rsqrt(jnp.mean(x * x, axis=-1, keepdims=True) + EPS) * g


def _dot(a, b, **kw):
    return jnp.dot(a, b, preferred_element_type=F32, **kw)


def _dot_nt(a, b, **kw):
    return lax.dot_general(a, b, (((1,), (1,)), ((), ())), preferred_element_type=F32, **kw)


def _dot_tn(a, b, **kw):
    return lax.dot_general(a, b, (((0,), (0,)), ((), ())), preferred_element_type=F32, **kw)


D_MODEL = 2048
BRANCH_W = 1024
Z_GATES = 0
Z_QA = 3 * D_MODEL
Z_KA = Z_QA + 512
Z_VA = Z_KA + 512
Z_RA = Z_VA + 1024
Z_GU = Z_RA + 1024
Z_GG = Z_GU + 1024
Z_CQ = Z_GG + 1024
Z_CKV = Z_CQ + 512
Z_KR = Z_CKV + 256
Z_ALR = Z_KR + 128
Z_DIM = Z_ALR + 128


def _rearrange_in_proj(w, b):
    def cols(a, s, n):
        return a[..., s:s + n]
    half = MLA_ROPE // 2
    def build(a):
        kr = cols(a, 5904, 64)
        pad = jnp.zeros(a.shape[:-1] + (LANE - GLA_RANK,), a.dtype)
        return jnp.concatenate([
            cols(a, 5968, 6144),
            cols(a, 0, 512), cols(a, 512, 512),
            cols(a, 1024, 1024), cols(a, 2048, 1024),
            cols(a, 3088, 1024), cols(a, 4112, 1024),
            cols(a, 5136, 512), cols(a, 5648, 256),
            kr, kr[..., half:], kr[..., :half],
            cols(a, 3072, GLA_RANK), pad], axis=-1)
    return build(w).astype(BF16), build(b)[None, :]


def _in_proj_kernel(x_ref, g_ref, w_ref, b_ref, o_ref, h_ref):
    @pl.when(pl.program_id(1) == 0)
    def _():
        h_ref[...] = _rms(x_ref[...], g_ref[...]).astype(BF16)
    o_ref[...] = _dot(h_ref[...], w_ref[...]) + b_ref[...]


def in_proj(x, g, w, b, *, tm=1024, tn=512):
    n, d = x.shape
    zd = w.shape[1]
    tm = min(tm, n)
    return pl.pallas_call(
        _in_proj_kernel,
        out_shape=jax.ShapeDtypeStruct((n, zd), F32),
        grid=(n // tm, zd // tn),
        in_specs=[pl.BlockSpec((tm, d), lambda i, j: (i, 0)),
                  pl.BlockSpec((1, d), lambda i, j: (0, 0)),
                  pl.BlockSpec((d, tn), lambda i, j: (0, j)),
                  pl.BlockSpec((1, tn), lambda i, j: (0, j))],
        out_specs=pl.BlockSpec((tm, tn), lambda i, j: (i, j)),
        scratch_shapes=[pltpu.VMEM((tm, d), BF16)],
        compiler_params=_cp(("parallel", "arbitrary")),
        name="in_proj",
    )(x, g, w, b)


def _gla_kernel(q_ref, k_ref, v_ref, r_ref, alr_ref, wa2_ref, ba_ref, gg_ref, s0_ref,
                o_ref, sout_ref, s_scr, *, chunk, n_chunks, nb, t_valid):
    t = pl.program_id(2)

    @pl.when(t == 0)
    def _():
        s_scr[...] = s0_ref[:, 0]

    C = chunk
    rows_per_seq = t_valid if t_valid < C else n_chunks * C
    row = lax.broadcasted_iota(jnp.int32, (C, C), 0)
    col = lax.broadcasted_iota(jnp.int32, (C, C), 1)
    tril = (row >= col)
    tril_f = tril.astype(F32)
    ones_cl = jnp.ones((C, LANE), F32)
    wa2 = wa2_ref[...]
    ba = ba_ref[...]
    gg = gg_ref[...]

    def pad_rows(a):
        if t_valid < C:
            return jnp.concatenate([a, jnp.zeros((C - t_valid, a.shape[1]), a.dtype)], axis=0)
        return a

    for i in range(nb):
        S = s_scr[i]
        for c in range(n_chunks):
            r0 = i * rows_per_seq + c * C
            nrow = min(C, rows_per_seq)
            sl = pl.ds(r0, nrow)
            la = _log_sigmoid(_dot(alr_ref[sl, :], wa2, precision=HI) + ba) * (1.0 / GLA_TAU)
            la = pad_rows(la)
            q = pad_rows(q_ref[sl, :]) * (GLA_DK ** -0.5)
            k = pad_rows(k_ref[sl, :])
            v = pad_rows(v_ref[sl, :]).astype(BF16)
            b = _dot(tril_f, la, precision=HI)
            b_end = b[C - 1:C, :]
            qd = (q * jnp.exp(b)).astype(BF16)
            ki = (k * jnp.exp(-b)).astype(BF16)
            kd = (k * jnp.exp(b_end - b)).astype(BF16)
            att = jnp.where(tril, _dot_nt(qd, ki), 0.0).astype(BF16)
            o = _dot(qd, S.astype(BF16)) + _dot(att, v)
            bcol = _dot_tn(la, ones_cl, precision=HI)
            dec = jnp.exp(bcol)
            S = jnp.concatenate([dec, dec], axis=1) * S + _dot_tn(kd, v)
            on = _rms(o, gg)
            rg = r_ref[sl, :]
            og = on[:nrow] * (rg * _sigmoid(rg))
            o_ref[sl, :] = og.astype(o_ref.dtype)
        s_scr[i] = S

    @pl.when(t == pl.num_programs(2) - 1)
    def _():
        sout_ref[:, 0] = s_scr[...]


def gla_branch(z, wa2p, ba, gg, s0, *, batch, seq, chunk, tt, nb):
    n = z.shape[0]
    if seq < chunk:
        t_valid, n_chunks, nt, rows = seq, 1, 1, nb * seq
    else:
        t_valid, n_chunks, nt, rows = chunk, tt // chunk, seq // tt, tt
        assert nb == 1 and batch == 1
    kern = functools.partial(_gla_kernel, chunk=chunk, n_chunks=n_chunks, nb=nb, t_valid=t_valid)
    qo, ko, vo, ro, ao = Z_QA // 128, Z_KA // 128, Z_VA // 256, Z_RA // 256, Z_ALR // 128
    rowmap = lambda b, h, t: b * nt + t
    return pl.pallas_call(
        kern,
        out_shape=(jax.ShapeDtypeStruct((n, GLA_HEADS * GLA_DV), BF16),
                   jax.ShapeDtypeStruct((batch, GLA_HEADS, GLA_DK, GLA_DV), F32)),
        grid=(batch // nb, GLA_HEADS, nt),
        in_specs=[pl.BlockSpec((rows, 128), lambda b, h, t: (rowmap(b, h, t), qo + h)),
                  pl.BlockSpec((rows, 128), lambda b, h, t: (rowmap(b, h, t), ko + h)),
                  pl.BlockSpec((rows, 256), lambda b, h, t: (rowmap(b, h, t), vo + h)),
                  pl.BlockSpec((rows, 256), lambda b, h, t: (rowmap(b, h, t), ro + h)),
                  pl.BlockSpec((rows, 128), lambda b, h, t: (rowmap(b, h, t), ao)),
                  pl.BlockSpec((128, 128), lambda b, h, t: (0, h)),
                  pl.BlockSpec((1, 128), lambda b, h, t: (0, h)),
                  pl.BlockSpec((1, 256), lambda b, h, t: (0, h)),
                  pl.BlockSpec((nb, 1, GLA_DK, GLA_DV), lambda b, h, t: (b, h, 0, 0))],
        out_specs=(pl.BlockSpec((rows, 256), lambda b, h, t: (rowmap(b, h, t), h)),
                   pl.BlockSpec((nb, 1, GLA_DK, GLA_DV), lambda b, h, t: (b, h, 0, 0))),
        scratch_shapes=[pltpu.VMEM((nb, GLA_DK, GLA_DV), F32)],
        compiler_params=_cp(("parallel", "parallel", "arbitrary")),
        name="gla",
    )(z, z, z, z, z, wa2p, ba, gg, s0)


def _ln_swish(y, lg, lb):
    mu = jnp.mean(y, axis=-1, keepdims=True)
    yc = y - mu
    var = jnp.mean(yc * yc, axis=-1, keepdims=True)
    yn = yc * lax.rsqrt(var + EPS) * lg + lb
    return yn * _sigmoid(yn)


def _conv_prompt_kernel(u_ref, g_ref, w_ref, cb_ref, lg_ref, lb_ref, o_ref, st_ref, win, ybuf, *, tt):
    i = pl.program_id(0)
    ch = u_ref.shape[1]

    @pl.when(i == 0)
    def _():
        win[0:32, :] = jnp.zeros((32, ch), F32)

    g = g_ref[...]
    win[32:32 + tt, :] = u_ref[...] * _sigmoid(g)
    for c0 in range(0, ch, LANE):
        cs = slice(c0, c0 + LANE)
        acc = jnp.zeros((tt, LANE), F32)
        for j in range(CONV_WIDTH):
            acc = acc + w_ref[j:j + 1, cs] * win[2 + j:2 + j + tt, cs]
        ybuf[:, cs] = acc + cb_ref[:, cs]
    o_ref[...] = _ln_swish(ybuf[...], lg_ref[...], lb_ref[...]).astype(o_ref.dtype)

    @pl.when(i == pl.num_programs(0) - 1)
    def _():
        st_ref[0] = win[tt + 2:tt + 32, :]

    win[0:32, :] = win[tt:tt + 32, :]


def conv_prompt(z, w, cb, lg, lb, *, tt=256):
    n = z.shape[0]
    ch = BRANCH_W
    uo, go = Z_GU // ch, Z_GG // ch
    return pl.pallas_call(
        functools.partial(_conv_prompt_kernel, tt=tt),
        out_shape=(jax.ShapeDtypeStruct((n, ch), BF16),
                   jax.ShapeDtypeStruct((1, HIST, ch), F32)),
        grid=(n // tt,),
        in_specs=[pl.BlockSpec((tt, ch), lambda i: (i, uo)),
                  pl.BlockSpec((tt, ch), lambda i: (i, go)),
                  pl.BlockSpec((CONV_WIDTH, ch), lambda i: (0, 0)),
                  pl.BlockSpec((1, ch), lambda i: (0, 0)),
                  pl.BlockSpec((1, ch), lambda i: (0, 0)),
                  pl.BlockSpec((1, ch), lambda i: (0, 0))],
        out_specs=(pl.BlockSpec((tt, ch), lambda i: (i, 0)),
                   pl.BlockSpec((1, HIST, ch), lambda i: (0, 0, 0))),
        scratch_shapes=[pltpu.VMEM((tt + 32, ch), F32), pltpu.VMEM((tt, ch), F32)],
        compiler_params=_cp(("arbitrary",)),
        name="conv_prompt",
    )(z, z, w, cb, lg, lb)


def _conv_sample_kernel(u_ref, g_ref, buf_ref, w_ref, cb_ref, lg_ref, lb_ref, o_ref, st_ref, win, *, nb, ts):
    ch = u_ref.shape[1]
    g = g_ref[...]
    a = u_ref[...] * _sigmoid(g)
    for i in range(nb):
        win[i, 0:32, :] = buf_ref[i]
        win[i, 32:32 + ts, :] = a[i * ts:(i + 1) * ts]
        acc = jnp.zeros((ts, ch), F32)
        for j in range(CONV_WIDTH):
            acc = acc + w_ref[j:j + 1, :] * win[i, 2 + j:2 + j + ts, :]
        y = acc + cb_ref[...]
        o_ref[i * ts:(i + 1) * ts, :] = _ln_swish(y, lg_ref[...], lb_ref[...]).astype(o_ref.dtype)
        st_ref[i] = win[i, 2 + ts:32 + ts, :]


def conv_sample(z, bufp, w, cb, lg, lb, *, batch, ts, nb=8):
    n = z.shape[0]
    ch = BRANCH_W
    uo, go = Z_GU // ch, Z_GG // ch
    return pl.pallas_call(
        functools.partial(_conv_sample_kernel, nb=nb, ts=ts),
        out_shape=(jax.ShapeDtypeStruct((n, ch), BF16),
                   jax.ShapeDtypeStruct((batch, HIST, ch), F32)),
        grid=(batch // nb,),
        in_specs=[pl.BlockSpec((nb * ts, ch), lambda i: (i, uo)),
                  pl.BlockSpec((nb * ts, ch), lambda i: (i, go)),
                  pl.BlockSpec((nb, 32, ch), lambda i: (i, 0, 0)),
                  pl.BlockSpec((CONV_WIDTH, ch), lambda i: (0, 0)),
                  pl.BlockSpec((1, ch), lambda i: (0, 0)),
                  pl.BlockSpec((1, ch), lambda i: (0, 0)),
                  pl.BlockSpec((1, ch), lambda i: (0, 0))],
        out_specs=(pl.BlockSpec((nb * ts, ch), lambda i: (i, 0)),
                   pl.BlockSpec((nb, HIST, ch), lambda i: (i, 0, 0))),
        scratch_shapes=[pltpu.VMEM((nb, 32 + ts, ch), F32)],
        compiler_params=_cp(("parallel",)),
        name="conv_sample",
    )(z, z, bufp, w, cb, lg, lb)


def _rope_pair(v, gpair, tab):
    lane = lax.broadcasted_iota(jnp.int32, v.shape, 1)
    ms = jnp.sum(jnp.where(lane < MLA_ROPE, v * v, 0.0), axis=-1, keepdims=True) * (1.0 / MLA_ROPE)
    t = v * lax.rsqrt(ms + EPS) * gpair * tab
    return t + pltpu.roll(t, MLA_ROPE, axis=1)


def _mla_prep_kernel(cq_ref, ckv_ref, krp_ref, tab_ref, gcq_ref, gckv_ref, gkpe_ref, wuq_ref, gqn_ref, gqr_ref,
                     wk_ref, gkn_ref, *out_refs, sample):
    if sample:
        ckv_o, kpe_o, qa_o, qp_o = out_refs
    else:
        ckv_o, kpe_o, qa_o, kc_o, ckvb_o = out_refs
    tab = tab_ref[...]
    lane = lax.broadcasted_iota(jnp.int32, tab.shape, 1)
    ckv = _rms(ckv_ref[...], gckv_ref[...])
    ckv_o[...] = ckv
    kpe = _rope_pair(krp_ref[...], gkpe_ref[...], tab)
    kpe_o[...] = kpe[:, :MLA_ROPE]
    kpe_z = jnp.where(lane < MLA_ROPE, kpe, 0.0)
    cqn = _rms(cq_ref[...], gcq_ref[...]).astype(BF16)
    ckv_b = ckv.astype(BF16)
    if not sample:
        ckvb_o[...] = ckv_b
    for h in range(MLA_HEADS):
        qh = _dot(cqn, wuq_ref[h])
        qn = _rms(qh[:, :MLA_NOPE], gqn_ref[...])
        qp = _rope_pair(qh[:, MLA_NOPE:], gqr_ref[...], tab)
        qp_z = jnp.where(lane < MLA_ROPE, qp, 0.0) * MLA_SCALE
        if sample:
            qg = (qn * gkn_ref[...] * MLA_SCALE).astype(BF16)
            qa_o[:, h * 256:(h + 1) * 256] = _dot(qg, wk_ref[h]).astype(BF16)
            qp_o[:, h * LANE:(h + 1) * LANE] = qp_z.astype(BF16)
        else:
            qa_o[h, :, 0:LANE] = (qn * MLA_SCALE).astype(BF16)
            qa_o[h, :, LANE:2 * LANE] = qp_z.astype(BF16)
            kn = _rms(_dot(ckv_b, wk_ref[h]), gkn_ref[...])
            kc_o[h, :, 0:LANE] = kn.astype(BF16)
            kc_o[h, :, LANE:2 * LANE] = kpe_z.astype(BF16)


def mla_prep(z, tab, gcq, gckv, gkpe2, wuq, gqn, gqr2, wk, gkn, *, sample, tm=256):
    n = z.shape[0]
    tm = min(tm, n)
    H = MLA_HEADS
    full = lambda *s: pl.BlockSpec(s, lambda i: (0,) * len(s))
    in_specs = [pl.BlockSpec((tm, 512), lambda i: (i, Z_CQ // 512)),
                pl.BlockSpec((tm, 256), lambda i: (i, Z_CKV // 256)),
                pl.BlockSpec((tm, 128), lambda i: (i, Z_KR // 128)),
                pl.BlockSpec((tm, 128), lambda i: (i, 0)),
                full(1, 512), full(1, 256), full(1, 128),
                full(H, 512, 256), full(1, 128), full(1, 128),
                full(*wk.shape), full(1, 128)]
    if sample:
        out_shape = (jax.ShapeDtypeStruct((n, 256), F32), jax.ShapeDtypeStruct((n, MLA_ROPE), F32),
                     jax.ShapeDtypeStruct((n, H * 256), BF16), jax.ShapeDtypeStruct((n, H * LANE), BF16))
        out_specs = (pl.BlockSpec((tm, 256), lambda i: (i, 0)), pl.BlockSpec((tm, MLA_ROPE), lambda i: (i, 0)),
                     pl.BlockSpec((tm, H * 256), lambda i: (i, 0)), pl.BlockSpec((tm, H * LANE), lambda i: (i, 0)))
    else:
        out_shape = (jax.ShapeDtypeStruct((n, 256), F32), jax.ShapeDtypeStruct((n, MLA_ROPE), F32),
                     jax.ShapeDtypeStruct((H, n, 256), BF16), jax.ShapeDtypeStruct((H, n, 256), BF16),
                     jax.ShapeDtypeStruct((n, 256), BF16))
        out_specs = (pl.BlockSpec((tm, 256), lambda i: (i, 0)), pl.BlockSpec((tm, MLA_ROPE), lambda i: (i, 0)),
                     pl.BlockSpec((H, tm, 256), lambda i: (0, i, 0)), pl.BlockSpec((H, tm, 256), lambda i: (0, i, 0)),
                     pl.BlockSpec((tm, 256), lambda i: (i, 0)))
    return pl.pallas_call(
        functools.partial(_mla_prep_kernel, sample=sample),
        out_shape=out_shape, grid=(n // tm,), in_specs=in_specs, out_specs=out_specs,
        compiler_params=_cp(("parallel",)),
        name="mla_prep_sample" if sample else "mla_prep_prompt",
    )(z, z, z, tab, gcq, gckv, gkpe2, wuq, gqn, gqr2, wk, gkn)


def _flash_kernel(q_ref, k_ref, v_ref, wuv_ref, o_ref, m_scr, l_scr, acc_scr, *, tq):
    i = pl.program_id(0)
    j = pl.program_id(1)
    H = q_ref.shape[0]

    @pl.when(j == 0)
    def _():
        m_scr[...] = jnp.full(m_scr.shape, -jnp.inf, F32)
        l_scr[...] = jnp.zeros(l_scr.shape, F32)
        acc_scr[...] = jnp.zeros(acc_scr.shape, F32)

    def update(masked):
        v = v_ref[...]
        row = lax.broadcasted_iota(jnp.int32, (tq, tq), 0)
        col = lax.broadcasted_iota(jnp.int32, (tq, tq), 1)

        def body(h, carry):
            s = _dot_nt(q_ref[h], k_ref[h])
            if masked:
                s = jnp.where(col <= row, s, -jnp.inf)
            m_prev = m_scr[h]
            m_new = jnp.maximum(m_prev, jnp.max(s, axis=-1, keepdims=True))
            alpha = jnp.exp(m_prev - m_new)
            p = jnp.exp(s - m_new)
            l_scr[h] = alpha * l_scr[h] + jnp.sum(p, axis=-1, keepdims=True)
            acc_scr[h] = alpha * acc_scr[h] + _dot(p.astype(BF16), v)
            m_scr[h] = m_new
            return carry

        lax.fori_loop(0, H, body, 0)

    @pl.when(j < i)
    def _():
        update(False)

    @pl.when(j == i)
    def _():
        update(True)
        for h in range(H):
            o_lat = (acc_scr[h] / l_scr[h]).astype(BF16)
            o_ref[:, h * MLA_V:(h + 1) * MLA_V] = _dot(o_lat, wuv_ref[h]).astype(o_ref.dtype)


def flash_prompt(qc, kc, ckv_b, wuv, *, tq=512):
    H, n, dk = qc.shape
    nq = n // tq
    return pl.pallas_call(
        functools.partial(_flash_kernel, tq=tq),
        out_shape=jax.ShapeDtypeStruct((n, H * MLA_V), BF16),
        grid=(nq, nq),
        in_specs=[pl.BlockSpec((H, tq, dk), lambda i, j: (0, i, 0)),
                  pl.BlockSpec((H, tq, dk), lambda i, j: (0, jnp.minimum(i, j), 0)),
                  pl.BlockSpec((tq, 256), lambda i, j: (jnp.minimum(i, j), 0)),
                  pl.BlockSpec((H, 256, MLA_V), lambda i, j: (0, 0, 0))],
        out_specs=pl.BlockSpec((tq, H * MLA_V), lambda i, j: (i, 0)),
        scratch_shapes=[pltpu.VMEM((H, tq, 1), F32), pltpu.VMEM((H, tq, 1), F32), pltpu.VMEM((H, tq, 256), F32)],
        compiler_params=_cp(("parallel", "arbitrary")),
        name="flash_prompt",
    )(qc, kc, ckv_b, wuv)


PAGES_PER_STEP = 16
PAGES_PER_SUB = 4


def _attend_sample_kernel(pt_ref, *refs, ts, page):
    npg = PAGES_PER_STEP
    ckv_pages = refs[:npg]
    kpe_pages = refs[npg:2 * npg]
    (qa_ref, qp_ref, cnew_ref, pnew_ref, wkt_ref, wuv_ref, o_ref, lhs, m_scr, l_scr, acc_scr) = refs[2 * npg:]
    b = pl.program_id(0)
    j = pl.program_id(1)
    H = MLA_HEADS
    R = H * ts
    NK = H * MLA_NOPE

    @pl.when(jnp.logical_and(b == 0, j == 0))
    def _():
        lhs[0:NK, :] = wkt_ref[...]

    @pl.when(j == 0)
    def _():
        lhs[NK:NK + R, :] = qa_ref[0]
        m_scr[...] = jnp.full(m_scr.shape, -jnp.inf, F32)
        l_scr[...] = jnp.zeros(l_scr.shape, F32)
        acc_scr[...] = jnp.zeros(acc_scr.shape, F32)

    qp = qp_ref[0]

    def block(c_f32, kpe_f32, mask):
        n = c_f32.shape[0]
        cb = c_f32.astype(BF16)
        big = _dot_nt(lhs[...], cb)
        kn = big[0:NK].reshape(H, MLA_NOPE, n)
        rinv = lax.rsqrt(jnp.sum(kn * kn, axis=1) * (1.0 / MLA_NOPE) + EPS)
        s = big[NK:NK + R].reshape(H, ts, n) * rinv[:, None, :]
        s = s.reshape(R, n) + _dot_nt(qp[:, :MLA_ROPE], kpe_f32.astype(BF16))
        if mask is not None:
            s = jnp.where(mask, s, -jnp.inf)
        m_prev = m_scr[...]
        m_new = jnp.maximum(m_prev, jnp.max(s, axis=-1, keepdims=True))
        alpha = jnp.exp(m_prev - m_new)
        p = jnp.exp(s - m_new)
        l_scr[...] = alpha * l_scr[...] + jnp.sum(p, axis=-1, keepdims=True)
        acc_scr[...] = alpha * acc_scr[...] + _dot(p.astype(BF16), cb)
        m_scr[...] = m_new

    for s0 in range(0, npg, PAGES_PER_SUB):
        c = jnp.concatenate([ckv_pages[s0 + t][...] for t in range(PAGES_PER_SUB)], axis=0)
        kp = jnp.concatenate([kpe_pages[s0 + t][...] for t in range(PAGES_PER_SUB)], axis=0)
        block(c, kp, None)

    @pl.when(j == pl.num_programs(1) - 1)
    def _():
        npad = LANE
        c = jnp.concatenate([cnew_ref[...], jnp.zeros((npad - ts, 256), F32)], axis=0)
        kp = jnp.concatenate([pnew_ref[...], jnp.zeros((npad - ts, MLA_ROPE), F32)], axis=0)
        key = lax.broadcasted_iota(jnp.int32, (R, npad), 1)
        qi = lax.broadcasted_iota(jnp.int32, (R, npad), 0) % ts
        block(c, kp, key <= qi)
        o_lat = acc_scr[...] / l_scr[...]
        for h in range(H):
            o_ref[:, h * MLA_V:(h + 1) * MLA_V] = _dot(o_lat[h * ts:(h + 1) * ts].astype(BF16), wuv_ref[h])


def attend_sample(page_table, cache_ckv, cache_kpe, layer, qa, qp, ckv_new, kpe_new, wkt, wuv, *, batch, ts):
    npg = PAGES_PER_STEP
    page = cache_ckv.shape[2]
    n_pages = page_table.shape[1]
    H = MLA_HEADS
    R = H * ts

    def page_spec(width, p):
        return pl.BlockSpec((None, None, page, width), lambda b, j, pt: (layer, pt[b, j * npg + p], 0, 0))

    in_specs = ([page_spec(256, p) for p in range(npg)] + [page_spec(MLA_ROPE, p) for p in range(npg)] + [
        pl.BlockSpec((1, R, 256), lambda b, j, pt: (b, 0, 0)),
        pl.BlockSpec((1, R, LANE), lambda b, j, pt: (b, 0, 0)),
        pl.BlockSpec((ts, 256), lambda b, j, pt: (b, 0)),
        pl.BlockSpec((ts, MLA_ROPE), lambda b, j, pt: (b, 0)),
        pl.BlockSpec((H * MLA_NOPE, 256), lambda b, j, pt: (0, 0)),
        pl.BlockSpec((H, 256, MLA_V), lambda b, j, pt: (0, 0, 0))])
    grid_spec = pltpu.PrefetchScalarGridSpec(
        num_scalar_prefetch=1, grid=(batch, n_pages // npg), in_specs=in_specs,
        out_specs=pl.BlockSpec((ts, H * MLA_V), lambda b, j, pt: (b, 0)),
        scratch_shapes=[pltpu.VMEM((H * MLA_NOPE + R, 256), BF16), pltpu.VMEM((R, 1), F32),
                        pltpu.VMEM((R, 1), F32), pltpu.VMEM((R, 256), F32)])
    return pl.pallas_call(
        functools.partial(_attend_sample_kernel, ts=ts, page=page),
        out_shape=jax.ShapeDtypeStruct((batch * ts, H * MLA_V), F32),
        grid_spec=grid_spec,
        compiler_params=_cp(("arbitrary", "arbitrary")),
        name="attend_sample",
    )(page_table, *([cache_ckv] * npg), *([cache_kpe] * npg), qa, qp, ckv_new, kpe_new, wkt, wuv)


def _merge_kernel(oa_ref, ob_ref, oc_ref, ga_ref, gb_ref, gc_ref, wa_ref, wb_ref, wc_ref, m_ref):
    acc = _sigmoid(ga_ref[...]) * _dot(oa_ref[...].astype(BF16), wa_ref[0])
    acc = acc + _sigmoid(gb_ref[...]) * _dot(ob_ref[...].astype(BF16), wb_ref[0])
    acc = acc + _sigmoid(gc_ref[...]) * _dot(oc_ref[...].astype(BF16), wc_ref[0])
    m_ref[...] = acc.astype(m_ref.dtype)


def branch_merge(oa, ob, oc, z, wbr, *, tm=512, tn=512):
    n = oa.shape[0]
    tm = min(tm, n)
    d = wbr.shape[2]
    w = oa.shape[1]
    nj = d // tn
    ospec = pl.BlockSpec((tm, w), lambda i, j: (i, 0))
    gspec = lambda k: pl.BlockSpec((tm, tn), lambda i, j: (i, k * nj + j))
    wspec = lambda k: pl.BlockSpec((1, w, tn), lambda i, j: (k, 0, j))
    return pl.pallas_call(
        _merge_kernel,
        out_shape=jax.ShapeDtypeStruct((n, d), BF16),
        grid=(n // tm, nj),
        in_specs=[ospec, ospec, ospec, gspec(0), gspec(1), gspec(2), wspec(0), wspec(1), wspec(2)],
        out_specs=pl.BlockSpec((tm, tn), lambda i, j: (i, j)),
        compiler_params=_cp(("parallel", "arbitrary")),
        name="branch_merge",
    )(oa, ob, oc, z, z, z, wbr, wbr, wbr)


def _out_proj_kernel(m_ref, w_ref, x_ref, o_ref):
    o_ref[...] = x_ref[...] + _dot(m_ref[...], w_ref[...])


def out_proj(m, w, x, *, tm=512, tn=512):
    n, d = x.shape
    tm = min(tm, n)
    return pl.pallas_call(
        _out_proj_kernel,
        out_shape=jax.ShapeDtypeStruct((n, d), F32),
        grid=(n // tm, d // tn),
        in_specs=[pl.BlockSpec((tm, d), lambda i, j: (i, 0)),
                  pl.BlockSpec((d, tn), lambda i, j: (0, j)),
                  pl.BlockSpec((tm, tn), lambda i, j: (i, j))],
        out_specs=pl.BlockSpec((tm, tn), lambda i, j: (i, j)),
        compiler_params=_cp(("parallel", "arbitrary")),
        name="out_proj",
    )(m, w, x)


def _ffn_kernel(x_ref, g_ref, wr_ref, wg_ref, wu_ref, wd_ref, o_ref, h_ref, gate_ref, *, routed):
    e = pl.program_id(1)
    f = pl.program_id(2)
    first = jnp.logical_and(e == 0, f == 0)

    @pl.when(first)
    def _():
        x = x_ref[...]
        h = _rms(x, g_ref[...])
        h_ref[...] = h.astype(BF16)
        o_ref[...] = x
        if routed:
            logits = _dot(h, wr_ref[...], precision=HI)
            lane = lax.broadcasted_iota(jnp.int32, logits.shape, 1)
            logits = jnp.where(lane < N_EXPERTS, logits, -jnp.inf)
            m1 = jnp.max(logits, axis=-1, keepdims=True)
            i1 = jnp.min(jnp.where(logits == m1, lane, LANE), axis=-1, keepdims=True)
            rest = jnp.where(lane == i1, -jnp.inf, logits)
            m2 = jnp.max(rest, axis=-1, keepdims=True)
            i2 = jnp.min(jnp.where(rest == m2, lane, LANE), axis=-1, keepdims=True)
            e2 = jnp.exp(m2 - m1)
            w1 = 1.0 / (1.0 + e2)
            w2 = e2 / (1.0 + e2)
            gate_ref[...] = jnp.where(lane == i1, w1, 0.0) + jnp.where(lane == i2, w2, 0.0)

    h = h_ref[...]
    a = _dot(h, wg_ref[0])
    u = _dot(h, wu_ref[0])
    act = a * _sigmoid(a) * u
    if routed:
        lane = lax.broadcasted_iota(jnp.int32, gate_ref.shape, 1)
        ge = jnp.sum(jnp.where(lane == e, gate_ref[...], 0.0), axis=-1, keepdims=True)
        act = act * ge
    o_ref[...] += _dot(act.astype(BF16), wd_ref[0])


def ffn(x, g, wr, wg, wu, wd, *, routed, tm=512, tf=512):
    n, d = x.shape
    E, _, dff = wg.shape
    tm = min(tm, n)
    return pl.pallas_call(
        functools.partial(_ffn_kernel, routed=routed),
        out_shape=jax.ShapeDtypeStruct((n, d), F32),
        grid=(n // tm, E, dff // tf),
        in_specs=[pl.BlockSpec((tm, d), lambda i, e, f: (i, 0)),
                  pl.BlockSpec((1, d), lambda i, e, f: (0, 0)),
                  pl.BlockSpec((d, LANE), lambda i, e, f: (0, 0)),
                  pl.BlockSpec((1, d, tf), lambda i, e, f: (e, 0, f)),
                  pl.BlockSpec((1, d, tf), lambda i, e, f: (e, 0, f)),
                  pl.BlockSpec((1, tf, d), lambda i, e, f: (e, f, 0))],
        out_specs=pl.BlockSpec((tm, d), lambda i, e, f: (i, 0)),
        scratch_shapes=[pltpu.VMEM((tm, d), BF16), pltpu.VMEM((tm, LANE), F32)],
        compiler_params=_cp(("parallel", "arbitrary", "arbitrary")),
        name="ffn_moe" if routed else "ffn_dense",
    )(x, g, wr, wg, wu, wd)


def _rope_table(pos):
    half = MLA_ROPE // 2
    inv = ROPE_THETA ** (-jnp.arange(half, dtype=F32) / half)
    ang = pos.astype(F32)[:, None] * inv[None, :]
    c, s = jnp.cos(ang), jnp.sin(ang)
    return jnp.concatenate([c, c, -s, s], axis=-1)


def _swap_pair(g):
    half = MLA_ROPE // 2
    return jnp.concatenate([g, g[half:], g[:half]])[None, :]


def kernel(x_prompt, x_sample, cache_kv_latent, cache_k_rope, state_gla, state_conv, page_table, g_mix, w_in, b_in, gla_w_a2, gla_b_a, gla_g_out, conv_w, conv_b, conv_ln_g, conv_ln_b, mla_g_cq, mla_g_ckv, mla_g_kpe, mla_w_uq, mla_g_qn, mla_g_qr, mla_w_uk, mla_g_kn, mla_w_uv, w_branch, w_out, g_ffn, ffn_w_gate, ffn_w_up, ffn_w_down, moe_w_router, moe_w_gate, moe_w_up, moe_w_down):
    depth = w_in.shape[0]
    bp, tp, d = x_prompt.shape
    bs, ts, _ = x_sample.shape
    assert bp == 1
    past = page_table.shape[1] * cache_kv_latent.shape[2]
    H = MLA_HEADS
    half = MLA_ROPE // 2

    xp = x_prompt.reshape(bp * tp, d)
    xs = x_sample.reshape(bs * ts, d)
    tab_p = _rope_table(jnp.arange(tp, dtype=jnp.int32))
    tab_s = jnp.tile(_rope_table(past + jnp.arange(ts, dtype=jnp.int32)), (bs, 1))
    s0_p = jnp.zeros((bp, GLA_HEADS, GLA_DK, GLA_DV), F32)

    outs = [[] for _ in range(8)]
    for l in range(depth):
        w_in_r, b_in_r = _rearrange_in_proj(w_in[l], b_in[l])
        g_mix_l = g_mix[l][None, :]
        wa2p = jnp.pad(gla_w_a2[l], ((0, LANE - GLA_RANK), (0, 0)))
        ba = gla_b_a[l][None, :]
        gg = gla_g_out[l].reshape(1, GLA_HEADS * GLA_DV)
        cw, cb = conv_w[l], conv_b[l][None, :]
        lg, lb = conv_ln_g[l][None, :], conv_ln_b[l][None, :]
        gcq, gckv = mla_g_cq[l][None, :], mla_g_ckv[l][None, :]
        gkpe2, gqr2 = _swap_pair(mla_g_kpe[l]), _swap_pair(mla_g_qr[l])
        gqn, gkn = mla_g_qn[l][None, :], mla_g_kn[l][None, :]
        wq = mla_w_uq[l]
        wq_rope = wq[..., MLA_NOPE:]
        wuq = jnp.concatenate([wq, wq_rope[..., half:], wq_rope[..., :half]], axis=-1)
        wuq = wuq.transpose(1, 0, 2).astype(BF16)
        wk = mla_w_uk[l].transpose(1, 0, 2).astype(BF16)
        wkT = mla_w_uk[l].transpose(1, 2, 0).astype(BF16)
        wuv = mla_w_uv[l].transpose(1, 0, 2).astype(BF16)
        wbr = w_branch[l].astype(BF16)
        wo = w_out[l].astype(BF16)

        zp = in_proj(xp, g_mix_l, w_in_r, b_in_r)
        oa_p, sg_p = gla_branch(zp, wa2p, ba, gg, s0_p, batch=bp, seq=tp, chunk=64, tt=512, nb=1)
        ob_p, sc_p = conv_prompt(zp, cw, cb, lg, lb)
        ckv_p, kpe_p, qc_p, kc_p, ckvb_p = mla_prep(zp, tab_p, gcq, gckv, gkpe2, wuq, gqn, gqr2, wk, gkn, sample=False)
        oc_p = flash_prompt(qc_p, kc_p, ckvb_p, wuv)
        xp = out_proj(branch_merge(oa_p, ob_p, oc_p, zp, wbr), wo, xp)

        zs = in_proj(xs, g_mix_l, w_in_r, b_in_r)
        oa_s, sg_s = gla_branch(zs, wa2p, ba, gg, state_gla[l], batch=bs, seq=ts, chunk=16, tt=ts, nb=8)
        bufp = jnp.pad(state_conv[l], ((0, 0), (2, 0), (0, 0)))
        ob_s, sc_s = conv_sample(zs, bufp, cw, cb, lg, lb, batch=bs, ts=ts)
        ckv_s, kpe_s, qa_s, qp_s = mla_prep(zs, tab_s, gcq, gckv, gkpe2, wuq, gqn, gqr2, wkT, gkn, sample=True)
        qa_s = qa_s.reshape(bs, ts, H, 256).transpose(0, 2, 1, 3).reshape(bs, H * ts, 256)
        qp_s = qp_s.reshape(bs, ts, H, LANE).transpose(0, 2, 1, 3).reshape(bs, H * ts, LANE)
        oc_s = attend_sample(page_table, cache_kv_latent, cache_k_rope, l, qa_s, qp_s, ckv_s, kpe_s,
                             wkT.reshape(H * MLA_NOPE, 256), wuv, batch=bs, ts=ts)
        xs = out_proj(branch_merge(oa_s, ob_s, oc_s, zs, wbr), wo, xs)

        gf = g_ffn[l][None, :]
        if l % 2 == 0:
            wr = jnp.zeros((d, LANE), F32)
            wg_, wu_, wd_ = (w[l // 2][None].astype(BF16) for w in (ffn_w_gate, ffn_w_up, ffn_w_down))
            routed = False
        else:
            wr = jnp.pad(moe_w_router[l // 2], ((0, 0), (0, LANE - N_EXPERTS)))
            wg_, wu_, wd_ = (w[l // 2].astype(BF16) for w in (moe_w_gate, moe_w_up, moe_w_down))
            routed = True
        xp = ffn(xp, gf, wr, wg_, wu_, wd_, routed=routed)
        xs = ffn(xs, gf, wr, wg_, wu_, wd_, routed=routed)

        for k, v in enumerate((ckv_p.reshape(bp, tp, -1), kpe_p.reshape(bp, tp, -1), sg_p, sc_p,
                               ckv_s.reshape(bs, ts, -1), kpe_s.reshape(bs, ts, -1), sg_s, sc_s)):
            outs[k].append(v)

    return (xp.reshape(bp, tp, d), xs.reshape(bs, ts, d)) + tuple(jnp.stack(o) for o in outs)
```

```python
import functools
import math

import numpy as np
import jax
import jax.numpy as jnp
from jax import lax
from jax.experimental import pallas as pl
from jax.experimental.pallas import tpu as pltpu

F32 = jnp.float32
BF16 = jnp.bfloat16
EPS = 1e-6
HI = lax.Precision.HIGHEST

GLA_HEADS = 4
GLA_DK = 128
GLA_DV = 256
GLA_RANK = 16
GLA_TAU = 16.0
CONV_WIDTH = 31
HIST = CONV_WIDTH - 1
MLA_HEADS = 8
MLA_NOPE = 128
MLA_ROPE = 64
MLA_V = 128
MLA_SCALE = (MLA_NOPE + MLA_ROPE) ** -0.5
ROPE_THETA = 10000.0
N_EXPERTS = 8
LANE = 128
VMEM_LIMIT = 56 * 1024 * 1024


def _cp(sem, vmem=VMEM_LIMIT):
    return pltpu.CompilerParams(dimension_semantics=sem, vmem_limit_bytes=vmem)


def _sigmoid(x):
    return 1.0 / (1.0 + jnp.exp(-x))


def _log_sigmoid(x):
    return jnp.minimum(x, 0.0) - jnp.log(1.0 + jnp.exp(-jnp.abs(x)))


def _rms(x, g):
    return x * lax.rsqrt(jnp.mean(x * x, axis=-1, keepdims=True) + EPS) * g


def _dot(a, b, **kw):
    return jnp.dot(a, b, preferred_element_type=F32, **kw)


def _dot_nt(a, b, **kw):
    return lax.dot_general(a, b, (((1,), (1,)), ((), ())), preferred_element_type=F32, **kw)


def _dot_tn(a, b, **kw):
    return lax.dot_general(a, b, (((0,), (0,)), ((), ())), preferred_element_type=F32, **kw)


D_MODEL = 2048
BRANCH_W = 1024
Z_GATES = 0
Z_QA = 3 * D_MODEL
Z_KA = Z_QA + 512
Z_VA = Z_KA + 512
Z_RA = Z_VA + 1024
Z_GU = Z_RA + 1024
Z_GG = Z_GU + 1024
Z_CQ = Z_GG + 1024
Z_CKV = Z_CQ + 512
Z_KR = Z_CKV + 256
Z_ALR = Z_KR + 128
Z_DIM = Z_ALR + 128


def _rearrange_in_proj(w, b):
    def cols(a, s, n):
        return a[..., s:s + n]
    half = MLA_ROPE // 2
    def build(a):
        kr = cols(a, 5904, 64)
        pad = jnp.zeros(a.shape[:-1] + (LANE - GLA_RANK,), a.dtype)
        return jnp.concatenate([
            cols(a, 5968, 6144),
            cols(a, 0, 512), cols(a, 512, 512),
            cols(a, 1024, 1024), cols(a, 2048, 1024),
            cols(a, 3088, 1024), cols(a, 4112, 1024),
            cols(a, 5136, 512), cols(a, 5648, 256),
            kr, kr[..., half:], kr[..., :half],
            cols(a, 3072, GLA_RANK), pad], axis=-1)
    return build(w).astype(BF16), build(b)[None, :]


def _in_proj_kernel(x_ref, g_ref, w_ref, b_ref, o_ref, h_ref):
    @pl.when(pl.program_id(1) == 0)
    def _():
        h_ref[...] = _rms(x_ref[...], g_ref[...]).astype(BF16)
    o_ref[...] = _dot(h_ref[...], w_ref[...]) + b_ref[...]


def in_proj(x, g, w, b, *, tm=1024, tn=512):
    n, d = x.shape
    zd = w.shape[1]
    tm = min(tm, n)
    return pl.pallas_call(
        _in_proj_kernel,
        out_shape=jax.ShapeDtypeStruct((n, zd), F32),
        grid=(n // tm, zd // tn),
        in_specs=[pl.BlockSpec((tm, d), lambda i, j: (i, 0)),
                  pl.BlockSpec((1, d), lambda i, j: (0, 0)),
                  pl.BlockSpec((d, tn), lambda i, j: (0, j)),
                  pl.BlockSpec((1, tn), lambda i, j: (0, j))],
        out_specs=pl.BlockSpec((tm, tn), lambda i, j: (i, j)),
        scratch_shapes=[pltpu.VMEM((tm, d), BF16)],
        compiler_params=_cp(("parallel", "arbitrary")),
        name="in_proj",
    )(x, g, w, b)


def _gla_kernel(q_ref, k_ref, v_ref, r_ref, alr_ref, wa2_ref, ba_ref, gg_ref, s0_ref,
                o_ref, sout_ref, s_scr, *, chunk, n_chunks, nb, t_valid):
    t = pl.program_id(2)

    @pl.when(t == 0)
    def _():
        s_scr[...] = s0_ref[:, 0]

    C = chunk
    rows_per_seq = t_valid if t_valid < C else n_chunks * C
    row = lax.broadcasted_iota(jnp.int32, (C, C), 0)
    col = lax.broadcasted_iota(jnp.int32, (C, C), 1)
    tril = (row >= col)
    tril_f = tril.astype(F32)
    ones_cl = jnp.ones((C, LANE), F32)
    wa2 = wa2_ref[...]
    ba = ba_ref[...]
    gg = gg_ref[...]

    def pad_rows(a):
        if t_valid < C:
            return jnp.concatenate([a, jnp.zeros((C - t_valid, a.shape[1]), a.dtype)], axis=0)
        return a

    for i in range(nb):
        S = s_scr[i]
        for c in range(n_chunks):
            r0 = i * rows_per_seq + c * C
            nrow = min(C, rows_per_seq)
            sl = pl.ds(r0, nrow)
            la = _log_sigmoid(_dot(alr_ref[sl, :], wa2, precision=HI) + ba) * (1.0 / GLA_TAU)
            la = pad_rows(la)
            q = pad_rows(q_ref[sl, :]) * (GLA_DK ** -0.5)
            k = pad_rows(k_ref[sl, :])
            v = pad_rows(v_ref[sl, :]).astype(BF16)
            b = _dot(tril_f, la, precision=HI)
            b_end = b[C - 1:C, :]
            qd = (q * jnp.exp(b)).astype(BF16)
            ki = (k * jnp.exp(-b)).astype(BF16)
            kd = (k * jnp.exp(b_end - b)).astype(BF16)
            att = jnp.where(tril, _dot_nt(qd, ki), 0.0).astype(BF16)
            o = _dot(qd, S.astype(BF16)) + _dot(att, v)
            bcol = _dot_tn(la, ones_cl, precision=HI)
            dec = jnp.exp(bcol)
            S = jnp.concatenate([dec, dec], axis=1) * S + _dot_tn(kd, v)
            on = _rms(o, gg)
            rg = r_ref[sl, :]
            og = on[:nrow] * (rg * _sigmoid(rg))
            o_ref[sl, :] = og.astype(o_ref.dtype)
        s_scr[i] = S

    @pl.when(t == pl.num_programs(2) - 1)
    def _():
        sout_ref[:, 0] = s_scr[...]


def gla_branch(z, wa2p, ba, gg, s0, *, batch, seq, chunk, tt, nb):
    n = z.shape[0]
    if seq < chunk:
        t_valid, n_chunks, nt, rows = seq, 1, 1, nb * seq
    else:
        t_valid, n_chunks, nt, rows = chunk, tt // chunk, seq // tt, tt
        assert nb == 1 and batch == 1
    kern = functools.partial(_gla_kernel, chunk=chunk, n_chunks=n_chunks, nb=nb, t_valid=t_valid)
    qo, ko, vo, ro, ao = Z_QA // 128, Z_KA // 128, Z_VA // 256, Z_RA // 256, Z_ALR // 128
    rowmap = lambda b, h, t: b * nt + t
    return pl.pallas_call(
        kern,
        out_shape=(jax.ShapeDtypeStruct((n, GLA_HEADS * GLA_DV), BF16),
                   jax.ShapeDtypeStruct((batch, GLA_HEADS, GLA_DK, GLA_DV), F32)),
        grid=(batch // nb, GLA_HEADS, nt),
        in_specs=[pl.BlockSpec((rows, 128), lambda b, h, t: (rowmap(b, h, t), qo + h)),
                  pl.BlockSpec((rows, 128), lambda b, h, t: (rowmap(b, h, t), ko + h)),
                  pl.BlockSpec((rows, 256), lambda b, h, t: (rowmap(b, h, t), vo + h)),
                  pl.BlockSpec((rows, 256), lambda b, h, t: (rowmap(b, h, t), ro + h)),
                  pl.BlockSpec((rows, 128), lambda b, h, t: (rowmap(b, h, t), ao)),
                  pl.BlockSpec((128, 128), lambda b, h, t: (0, h)),
                  pl.BlockSpec((1, 128), lambda b, h, t: (0, h)),
                  pl.BlockSpec((1, 256), lambda b, h, t: (0, h)),
                  pl.BlockSpec((nb, 1, GLA_DK, GLA_DV), lambda b, h, t: (b, h, 0, 0))],
        out_specs=(pl.BlockSpec((rows, 256), lambda b, h, t: (rowmap(b, h, t), h)),
                   pl.BlockSpec((nb, 1, GLA_DK, GLA_DV), lambda b, h, t: (b, h, 0, 0))),
        scratch_shapes=[pltpu.VMEM((nb, GLA_DK, GLA_DV), F32)],
        compiler_params=_cp(("parallel", "parallel", "arbitrary")),
        name="gla",
    )(z, z, z, z, z, wa2p, ba, gg, s0)


def _ln_swish(y, lg, lb):
    mu = jnp.mean(y, axis=-1, keepdims=True)
    yc = y - mu
    var = jnp.mean(yc * yc, axis=-1, keepdims=True)
    yn = yc * lax.rsqrt(var + EPS) * lg + lb
    return yn * _sigmoid(yn)


def _conv_prompt_kernel(u_ref, g_ref, w_ref, cb_ref, lg_ref, lb_ref, o_ref, st_ref, win, ybuf, *, tt):
    i = pl.program_id(0)
    ch = u_ref.shape[1]

    @pl.when(i == 0)
    def _():
        win[0:32, :] = jnp.zeros((32, ch), F32)

    g = g_ref[...]
    win[32:32 + tt, :] = u_ref[...] * _sigmoid(g)
    for c0 in range(0, ch, LANE):
        cs = slice(c0, c0 + LANE)
        acc = jnp.zeros((tt, LANE), F32)
        for j in range(CONV_WIDTH):
            acc = acc + w_ref[j:j + 1, cs] * win[2 + j:2 + j + tt, cs]
        ybuf[:, cs] = acc + cb_ref[:, cs]
    o_ref[...] = _ln_swish(ybuf[...], lg_ref[...], lb_ref[...]).astype(o_ref.dtype)

    @pl.when(i == pl.num_programs(0) - 1)
    def _():
        st_ref[0] = win[tt + 2:tt + 32, :]

    win[0:32, :] = win[tt:tt + 32, :]


def conv_prompt(z, w, cb, lg, lb, *, tt=256):
    n = z.shape[0]
    ch = BRANCH_W
    uo, go = Z_GU // ch, Z_GG // ch
    return pl.pallas_call(
        functools.partial(_conv_prompt_kernel, tt=tt),
        out_shape=(jax.ShapeDtypeStruct((n, ch), BF16),
                   jax.ShapeDtypeStruct((1, HIST, ch), F32)),
        grid=(n // tt,),
        in_specs=[pl.BlockSpec((tt, ch), lambda i: (i, uo)),
                  pl.BlockSpec((tt, ch), lambda i: (i, go)),
                  pl.BlockSpec((CONV_WIDTH, ch), lambda i: (0, 0)),
                  pl.BlockSpec((1, ch), lambda i: (0, 0)),
                  pl.BlockSpec((1, ch), lambda i: (0, 0)),
                  pl.BlockSpec((1, ch), lambda i: (0, 0))],
        out_specs=(pl.BlockSpec((tt, ch), lambda i: (i, 0)),
                   pl.BlockSpec((1, HIST, ch), lambda i: (0, 0, 0))),
        scratch_shapes=[pltpu.VMEM((tt + 32, ch), F32), pltpu.VMEM((tt, ch), F32)],
        compiler_params=_cp(("arbitrary",)),
        name="conv_prompt",
    )(z, z, w, cb, lg, lb)


def _conv_sample_kernel(u_ref, g_ref, buf_ref, w_ref, cb_ref, lg_ref, lb_ref, o_ref, st_ref, win, *, nb, ts):
    ch = u_ref.shape[1]
    g = g_ref[...]
    a = u_ref[...] * _sigmoid(g)
    for i in range(nb):
        win[i, 0:32, :] = buf_ref[i]
        win[i, 32:32 + ts, :] = a[i * ts:(i + 1) * ts]
        acc = jnp.zeros((ts, ch), F32)
        for j in range(CONV_WIDTH):
            acc = acc + w_ref[j:j + 1, :] * win[i, 2 + j:2 + j + ts, :]
        y = acc + cb_ref[...]
        o_ref[i * ts:(i + 1) * ts, :] = _ln_swish(y, lg_ref[...], lb_ref[...]).astype(o_ref.dtype)
        st_ref[i] = win[i, 2 + ts:32 + ts, :]


def conv_sample(z, bufp, w, cb, lg, lb, *, batch, ts, nb=8):
    n = z.shape[0]
    ch = BRANCH_W
    uo, go = Z_GU // ch, Z_GG // ch
    return pl.pallas_call(
        functools.partial(_conv_sample_kernel, nb=nb, ts=ts),
        out_shape=(jax.ShapeDtypeStruct((n, ch), BF16),
                   jax.ShapeDtypeStruct((batch, HIST, ch), F32)),
        grid=(batch // nb,),
        in_specs=[pl.BlockSpec((nb * ts, ch), lambda i: (i, uo)),
                  pl.BlockSpec((nb * ts, ch), lambda i: (i, go)),
                  pl.BlockSpec((nb, 32, ch), lambda i: (i, 0, 0)),
                  pl.BlockSpec((CONV_WIDTH, ch), lambda i: (0, 0)),
                  pl.BlockSpec((1, ch), lambda i: (0, 0)),
                  pl.BlockSpec((1, ch), lambda i: (0, 0)),
                  pl.BlockSpec((1, ch), lambda i: (0, 0))],
        out_specs=(pl.BlockSpec((nb * ts, ch), lambda i: (i, 0)),
                   pl.BlockSpec((nb, HIST, ch), lambda i: (i, 0, 0))),
        scratch_shapes=[pltpu.VMEM((nb, 32 + ts, ch), F32)],
        compiler_params=_cp(("parallel",)),
        name="conv_sample",
    )(z, z, bufp, w, cb, lg, lb)


def _rope_pair(v, gpair, tab):
    lane = lax.broadcasted_iota(jnp.int32, v.shape, 1)
    ms = jnp.sum(jnp.where(lane < MLA_ROPE, v * v, 0.0), axis=-1, keepdims=True) * (1.0 / MLA_ROPE)
    t = v * lax.rsqrt(ms + EPS) * gpair * tab
    return t + pltpu.roll(t, MLA_ROPE, axis=1)


def _mla_prep_kernel(cq_ref, ckv_ref, krp_ref, tab_ref, gcq_ref, gckv_ref, gkpe_ref, wuq_ref, gqn_ref, gqr_ref,
                     wk_ref, gkn_ref, *out_refs, sample):
    if sample:
        ckv_o, kpe_o, qa_o, qp_o = out_refs
    else:
        ckv_o, kpe_o, qa_o, kc_o, ckvb_o = out_refs
    tab = tab_ref[...]
    lane = lax.broadcasted_iota(jnp.int32, tab.shape, 1)
    ckv = _rms(ckv_ref[...], gckv_ref[...])
    ckv_o[...] = ckv
    kpe = _rope_pair(krp_ref[...], gkpe_ref[...], tab)
    kpe_o[...] = kpe[:, :MLA_ROPE]
    kpe_z = jnp.where(lane < MLA_ROPE, kpe, 0.0)
    cqn = _rms(cq_ref[...], gcq_ref[...]).astype(BF16)
    ckv_b = ckv.astype(BF16)
    if not sample:
        ckvb_o[...] = ckv.T.astype(BF16)
    for h in range(MLA_HEADS):
        qh = _dot(cqn, wuq_ref[h])
        qn = _rms(qh[:, :MLA_NOPE], gqn_ref[...])
        qp = _rope_pair(qh[:, MLA_NOPE:], gqr_ref[...], tab)
        qp_z = jnp.where(lane < MLA_ROPE, qp, 0.0) * MLA_SCALE
        if sample:
            qg = (qn * gkn_ref[...] * MLA_SCALE).astype(BF16)
            qa_o[:, h * 256:(h + 1) * 256] = _dot(qg, wk_ref[h]).astype(BF16)
            qp_o[:, h * LANE:(h + 1) * LANE] = qp_z.astype(BF16)
        else:
            qa_o[h, 0:LANE, :] = (qn * MLA_SCALE).T.astype(BF16)
            qa_o[h, LANE:2 * LANE, :] = qp_z.T.astype(BF16)
            kn = _rms(_dot(ckv_b, wk_ref[h]), gkn_ref[...])
            kc_o[h, :, 0:LANE] = kn.astype(BF16)
            kc_o[h, :, LANE:2 * LANE] = kpe_z.astype(BF16)


def mla_prep(z, tab, gcq, gckv, gkpe2, wuq, gqn, gqr2, wk, gkn, *, sample, tm=256):
    n = z.shape[0]
    tm = min(tm, n)
    H = MLA_HEADS
    full = lambda *s: pl.BlockSpec(s, lambda i: (0,) * len(s))
    in_specs = [pl.BlockSpec((tm, 512), lambda i: (i, Z_CQ // 512)),
                pl.BlockSpec((tm, 256), lambda i: (i, Z_CKV // 256)),
                pl.BlockSpec((tm, 128), lambda i: (i, Z_KR // 128)),
                pl.BlockSpec((tm, 128), lambda i: (i, 0)),
                full(1, 512), full(1, 256), full(1, 128),
                full(H, 512, 256), full(1, 128), full(1, 128),
                full(*wk.shape), full(1, 128)]
    if sample:
        out_shape = (jax.ShapeDtypeStruct((n, 256), F32), jax.ShapeDtypeStruct((n, MLA_ROPE), F32),
                     jax.ShapeDtypeStruct((n, H * 256), BF16), jax.ShapeDtypeStruct((n, H * LANE), BF16))
        out_specs = (pl.BlockSpec((tm, 256), lambda i: (i, 0)), pl.BlockSpec((tm, MLA_ROPE), lambda i: (i, 0)),
                     pl.BlockSpec((tm, H * 256), lambda i: (i, 0)), pl.BlockSpec((tm, H * LANE), lambda i: (i, 0)))
    else:
        out_shape = (jax.ShapeDtypeStruct((n, 256), F32), jax.ShapeDtypeStruct((n, MLA_ROPE), F32),
                     jax.ShapeDtypeStruct((H, 256, n), BF16), jax.ShapeDtypeStruct((H, n, 256), BF16),
                     jax.ShapeDtypeStruct((256, n), BF16))
        out_specs = (pl.BlockSpec((tm, 256), lambda i: (i, 0)), pl.BlockSpec((tm, MLA_ROPE), lambda i: (i, 0)),
                     pl.BlockSpec((H, 256, tm), lambda i: (0, 0, i)), pl.BlockSpec((H, tm, 256), lambda i: (0, i, 0)),
                     pl.BlockSpec((256, tm), lambda i: (0, i)))
    return pl.pallas_call(
        functools.partial(_mla_prep_kernel, sample=sample),
        out_shape=out_shape, grid=(n // tm,), in_specs=in_specs, out_specs=out_specs,
        compiler_params=_cp(("parallel",)),
        name="mla_prep_sample" if sample else "mla_prep_prompt",
    )(z, z, z, tab, gcq, gckv, gkpe2, wuq, gqn, gqr2, wk, gkn)


def _flash_kernel(qT_ref, k_ref, vT_ref, wuvT_ref, o_ref, m_scr, l_scr, acc_scr, *, tq):
    i = pl.program_id(0)
    j = pl.program_id(1)
    H = qT_ref.shape[0]

    @pl.when(j == 0)
    def _():
        m_scr[...] = jnp.full(m_scr.shape, -jnp.inf, F32)
        l_scr[...] = jnp.zeros(l_scr.shape, F32)
        acc_scr[...] = jnp.zeros(acc_scr.shape, F32)

    def update(masked):
        vT = vT_ref[...]
        krow = lax.broadcasted_iota(jnp.int32, (tq, tq), 0)
        qcol = lax.broadcasted_iota(jnp.int32, (tq, tq), 1)
        for h in range(H):
            sT = _dot(k_ref[h], qT_ref[h])
            if masked:
                sT = jnp.where(krow <= qcol, sT, -jnp.inf)
            m_prev = m_scr[h]
            m_new = jnp.maximum(m_prev, jnp.max(sT, axis=0, keepdims=True))
            alpha = jnp.exp(m_prev - m_new)
            pT = jnp.exp(sT - m_new)
            l_scr[h] = alpha * l_scr[h] + jnp.sum(pT, axis=0, keepdims=True)
            acc_scr[h] = alpha * acc_scr[h] + _dot(vT, pT.astype(BF16))
            m_scr[h] = m_new

    @pl.when(j < i)
    def _():
        update(False)

    @pl.when(j == i)
    def _():
        update(True)
        for h in range(H):
            o_latT = (acc_scr[h] / l_scr[h]).astype(BF16)
            oT = _dot(wuvT_ref[h], o_latT)
            o_ref[:, h * MLA_V:(h + 1) * MLA_V] = oT.T.astype(o_ref.dtype)


def flash_prompt(qcT, kc, ckvT_b, wuvT, *, tq=1024):
    H, dk, n = qcT.shape
    tq = min(tq, n)
    nq = n // tq
    return pl.pallas_call(
        functools.partial(_flash_kernel, tq=tq),
        out_shape=jax.ShapeDtypeStruct((n, H * MLA_V), BF16),
        grid=(nq, nq),
        in_specs=[pl.BlockSpec((H, dk, tq), lambda i, j: (0, 0, i)),
                  pl.BlockSpec((H, tq, dk), lambda i, j: (0, jnp.minimum(i, j), 0)),
                  pl.BlockSpec((256, tq), lambda i, j: (0, jnp.minimum(i, j))),
                  pl.BlockSpec((H, MLA_V, 256), lambda i, j: (0, 0, 0))],
        out_specs=pl.BlockSpec((tq, H * MLA_V), lambda i, j: (i, 0)),
        scratch_shapes=[pltpu.VMEM((H, 1, tq), F32), pltpu.VMEM((H, 1, tq), F32), pltpu.VMEM((H, 256, tq), F32)],
        compiler_params=_cp(("parallel", "arbitrary")),
        name="flash_prompt",
    )(qcT, kc, ckvT_b, wuvT)


ATTEND_PAGES_PER_STEP = 32
ATTEND_SUB_KEYS = 1024


def _attend_sample_kernel(pt_ref, ckv_hbm, kpeT_hbm, qa_ref, qp_ref, cnew_ref, pnewT_ref, wkt_ref, wuv_ref, o_ref,
                          cbuf, kbuf, lhs, m_scr, l_scr, acc_scr, sem, *, layer, ts, npg, n, page):
    b = pl.program_id(0)
    j = pl.program_id(1)
    nb = pl.num_programs(0)
    nj = pl.num_programs(1)
    step = b * nj + j
    slot = step % 2
    H = MLA_HEADS
    R = H * ts
    NK = H * MLA_NOPE
    tk = npg * page
    ppn = n // page

    def page_copies(bb, jj, sl):
        out = []
        for p in range(npg):
            pg = pt_ref[bb, jj * npg + p]
            out.append(pltpu.make_async_copy(ckv_hbm.at[layer, pg], cbuf.at[sl, p], sem.at[sl, 0]))
            out.append(pltpu.make_async_copy(kpeT_hbm.at[layer, pg], kbuf.at[sl, p], sem.at[sl, 1]))
        return out

    @pl.when(step == 0)
    def _():
        lhs[0:NK, :] = wkt_ref[...]
        for c in page_copies(0, 0, 0):
            c.start()

    @pl.when(step + 1 < nb * nj)
    def _():
        last_j = j + 1 == nj
        for c in page_copies(jnp.where(last_j, b + 1, b), jnp.where(last_j, 0, j + 1), 1 - slot):
            c.start()

    pltpu.make_async_copy(ckv_hbm.at[layer, pl.ds(0, npg)], cbuf.at[slot], sem.at[slot, 0]).wait()
    pltpu.make_async_copy(kpeT_hbm.at[layer, pl.ds(0, npg)], kbuf.at[slot], sem.at[slot, 1]).wait()

    @pl.when(j == 0)
    def _():
        lhs[NK:NK + R, :] = qa_ref[0]
        m_scr[...] = jnp.full(m_scr.shape, -jnp.inf, F32)
        l_scr[...] = jnp.zeros(l_scr.shape, F32)
        acc_scr[...] = jnp.zeros(acc_scr.shape, F32)

    qp = qp_ref[0][:, :MLA_ROPE]

    def scores(cb, kpT_b):
        m = cb.shape[0]
        big = _dot_nt(lhs[...], cb)
        kn = big[0:NK].reshape(H, MLA_NOPE, m)
        rinv = lax.rsqrt(jnp.sum(kn * kn, axis=1) * (1.0 / MLA_NOPE) + EPS)
        s = big[NK:NK + R].reshape(H, ts, m) * rinv[:, None, :]
        return s.reshape(R, m) + _dot(qp, kpT_b)

    def update(s, cb):
        m_prev = m_scr[...]
        m_new = jnp.maximum(m_prev, jnp.max(s, axis=-1, keepdims=True))
        alpha = jnp.exp(m_prev - m_new)
        p = jnp.exp(s - m_new)
        l_scr[...] = alpha * l_scr[...] + jnp.sum(p, axis=-1, keepdims=True)
        acc_scr[...] = alpha * acc_scr[...] + _dot(p.astype(BF16), cb)
        m_scr[...] = m_new

    prev = None
    for i in range(tk // n):
        cb = cbuf[slot, i * ppn:(i + 1) * ppn].reshape(n, 256).astype(BF16)
        kpb = jnp.concatenate([kbuf[slot, i * ppn + t] for t in range(ppn)], axis=1).astype(BF16)
        s = scores(cb, kpb)
        if prev is not None:
            update(*prev)
        prev = (s, cb)
    update(*prev)

    @pl.when(j == nj - 1)
    def _():
        c = jnp.concatenate([cnew_ref[...], jnp.zeros((LANE - ts, 256), F32)], axis=0).astype(BF16)
        key = lax.broadcasted_iota(jnp.int32, (R, LANE), 1)
        qi = lax.broadcasted_iota(jnp.int32, (R, LANE), 0) % ts
        s = jnp.where(key <= qi, scores(c, pnewT_ref[0].astype(BF16)), -jnp.inf)
        update(s, c)
        o_lat = acc_scr[...] / l_scr[...]
        for h in range(H):
            o_ref[:, h * MLA_V:(h + 1) * MLA_V] = _dot(o_lat[h * ts:(h + 1) * ts].astype(BF16), wuv_ref[h])


def attend_sample(page_table, cache_ckv, cache_kpeT, layer, qa, qp, ckv_new, kpe_newT, wkt, wuv, *, batch, ts):
    page = cache_ckv.shape[2]
    n_pages = page_table.shape[1]
    npg = math.gcd(ATTEND_PAGES_PER_STEP, n_pages)
    n = min(ATTEND_SUB_KEYS, npg * page)
    H = MLA_HEADS
    R = H * ts
    in_specs = [pl.BlockSpec(memory_space=pl.ANY), pl.BlockSpec(memory_space=pl.ANY),
                pl.BlockSpec((1, R, 256), lambda b, j, pt: (b, 0, 0)),
                pl.BlockSpec((1, R, LANE), lambda b, j, pt: (b, 0, 0)),
                pl.BlockSpec((ts, 256), lambda b, j, pt: (b, 0)),
                pl.BlockSpec((1, MLA_ROPE, LANE), lambda b, j, pt: (b, 0, 0)),
                pl.BlockSpec((H * MLA_NOPE, 256), lambda b, j, pt: (0, 0)),
                pl.BlockSpec((H, 256, MLA_V), lambda b, j, pt: (0, 0, 0))]
    grid_spec = pltpu.PrefetchScalarGridSpec(
        num_scalar_prefetch=1, grid=(batch, n_pages // npg), in_specs=in_specs,
        out_specs=pl.BlockSpec((ts, H * MLA_V), lambda b, j, pt: (b, 0)),
        scratch_shapes=[pltpu.VMEM((2, npg, page, 256), F32), pltpu.VMEM((2, npg, MLA_ROPE, page), F32),
                        pltpu.VMEM((H * MLA_NOPE + R, 256), BF16), pltpu.VMEM((R, 1), F32),
                        pltpu.VMEM((R, 1), F32), pltpu.VMEM((R, 256), F32), pltpu.SemaphoreType.DMA((2, 2))])
    return pl.pallas_call(
        functools.partial(_attend_sample_kernel, layer=layer, ts=ts, npg=npg, n=n, page=page),
        out_shape=jax.ShapeDtypeStruct((batch * ts, H * MLA_V), F32),
        grid_spec=grid_spec,
        compiler_params=_cp(("arbitrary", "arbitrary")),
        name="attend_sample",
    )(page_table, cache_ckv, cache_kpeT, qa, qp, ckv_new, kpe_newT, wkt, wuv)


def _merge_kernel(oa_ref, ob_ref, oc_ref, ga_ref, gb_ref, gc_ref, wa_ref, wb_ref, wc_ref, m_ref):
    acc = _sigmoid(ga_ref[...]) * _dot(oa_ref[...].astype(BF16), wa_ref[0])
    acc = acc + _sigmoid(gb_ref[...]) * _dot(ob_ref[...].astype(BF16), wb_ref[0])
    acc = acc + _sigmoid(gc_ref[...]) * _dot(oc_ref[...].astype(BF16), wc_ref[0])
    m_ref[...] = acc.astype(m_ref.dtype)


def branch_merge(oa, ob, oc, z, wbr, *, tm=512, tn=512):
    n = oa.shape[0]
    tm = min(tm, n)
    d = wbr.shape[2]
    w = oa.shape[1]
    nj = d // tn
    ospec = pl.BlockSpec((tm, w), lambda i, j: (i, 0))
    gspec = lambda k: pl.BlockSpec((tm, tn), lambda i, j: (i, k * nj + j))
    wspec = lambda k: pl.BlockSpec((1, w, tn), lambda i, j: (k, 0, j))
    return pl.pallas_call(
        _merge_kernel,
        out_shape=jax.ShapeDtypeStruct((n, d), BF16),
        grid=(n // tm, nj),
        in_specs=[ospec, ospec, ospec, gspec(0), gspec(1), gspec(2), wspec(0), wspec(1), wspec(2)],
        out_specs=pl.BlockSpec((tm, tn), lambda i, j: (i, j)),
        compiler_params=_cp(("parallel", "arbitrary")),
        name="branch_merge",
    )(oa, ob, oc, z, z, z, wbr, wbr, wbr)


def _out_proj_kernel(m_ref, w_ref, x_ref, o_ref):
    o_ref[...] = x_ref[...] + _dot(m_ref[...], w_ref[...])


def out_proj(m, w, x, *, tm=512, tn=512):
    n, d = x.shape
    tm = min(tm, n)
    return pl.pallas_call(
        _out_proj_kernel,
        out_shape=jax.ShapeDtypeStruct((n, d), F32),
        grid=(n // tm, d // tn),
        in_specs=[pl.BlockSpec((tm, d), lambda i, j: (i, 0)),
                  pl.BlockSpec((d, tn), lambda i, j: (0, j)),
                  pl.BlockSpec((tm, tn), lambda i, j: (i, j))],
        out_specs=pl.BlockSpec((tm, tn), lambda i, j: (i, j)),
        compiler_params=_cp(("parallel", "arbitrary")),
        name="out_proj",
    )(m, w, x)


def _swiglu_acc(h, wg, wu, wd):
    a = _dot(h, wg)
    u = _dot(h, wu)
    return _dot((a * _sigmoid(a) * u).astype(BF16), wd)


def _ffn_kernel(x_ref, g_ref, wg_ref, wu_ref, wd_ref, o_ref, h_ref):
    @pl.when(pl.program_id(1) == 0)
    def _():
        x = x_ref[...]
        h_ref[...] = _rms(x, g_ref[...]).astype(BF16)
        o_ref[...] = x
    o_ref[...] += _swiglu_acc(h_ref[...], wg_ref[...], wu_ref[...], wd_ref[...])


def ffn_dense(x, g, wg, wu, wd, *, tm=512, tf=512):
    n, d = x.shape
    dff = wg.shape[1]
    tm = min(tm, n)
    return pl.pallas_call(
        _ffn_kernel,
        out_shape=jax.ShapeDtypeStruct((n, d), F32),
        grid=(n // tm, dff // tf),
        in_specs=[pl.BlockSpec((tm, d), lambda i, f: (i, 0)),
                  pl.BlockSpec((1, d), lambda i, f: (0, 0)),
                  pl.BlockSpec((d, tf), lambda i, f: (0, f)),
                  pl.BlockSpec((d, tf), lambda i, f: (0, f)),
                  pl.BlockSpec((tf, d), lambda i, f: (f, 0))],
        out_specs=pl.BlockSpec((tm, d), lambda i, f: (i, 0)),
        scratch_shapes=[pltpu.VMEM((tm, d), BF16)],
        compiler_params=_cp(("parallel", "arbitrary")),
        name="ffn_dense",
    )(x, g, wg, wu, wd)


MOE_TILE = 512


def _route_kernel(x_ref, g_ref, wr_ref, h_ref, ti_ref, tw_ref):
    h = _rms(x_ref[...], g_ref[...])
    h_ref[...] = h.astype(BF16)
    logits = _dot(h, wr_ref[...], precision=HI)
    lane = lax.broadcasted_iota(jnp.int32, logits.shape, 1)
    logits = jnp.where(lane < N_EXPERTS, logits, -jnp.inf)
    m1 = jnp.max(logits, axis=-1, keepdims=True)
    i1 = jnp.min(jnp.where(logits == m1, lane, LANE), axis=-1, keepdims=True)
    rest = jnp.where(lane == i1, -jnp.inf, logits)
    m2 = jnp.max(rest, axis=-1, keepdims=True)
    i2 = jnp.min(jnp.where(rest == m2, lane, LANE), axis=-1, keepdims=True)
    e2 = jnp.exp(m2 - m1)
    w1 = 1.0 / (1.0 + e2)
    w2 = e2 / (1.0 + e2)
    ti_ref[...] = jnp.where(lane == 0, i1, jnp.where(lane == 1, i2, 0))
    tw_ref[...] = jnp.where(lane == 0, w1, jnp.where(lane == 1, w2, 0.0))


def moe_route(x, g, wr, *, tm=512):
    n, d = x.shape
    tm = min(tm, n)
    row = lambda w: pl.BlockSpec((tm, w), lambda i: (i, 0))
    return pl.pallas_call(
        _route_kernel,
        out_shape=(jax.ShapeDtypeStruct((n, d), BF16), jax.ShapeDtypeStruct((n, LANE), jnp.int32),
                   jax.ShapeDtypeStruct((n, LANE), F32)),
        grid=(n // tm,),
        in_specs=[row(d), pl.BlockSpec((1, d), lambda i: (0, 0)), pl.BlockSpec((d, LANE), lambda i: (0, 0))],
        out_specs=(row(d), row(LANE), row(LANE)),
        compiler_params=_cp(("parallel",)),
        name="moe_route",
    )(x, g, wr)


def _gather_rows_kernel(idx_ref, src_hbm, dst_hbm, sem, *, tile):
    t = pl.program_id(0)

    def issue(r, carry):
        row = idx_ref[0, 0, r]
        pltpu.make_async_copy(src_hbm.at[pl.ds(row, 1)], dst_hbm.at[pl.ds(t * tile + r, 1)], sem).start()
        return carry
    lax.fori_loop(0, tile, issue, 0)
    pltpu.make_async_copy(src_hbm.at[pl.ds(0, tile)], dst_hbm.at[pl.ds(t * tile, tile)], sem).wait()


def gather_rows(idx, src, *, tile):
    n_out = idx.shape[0]
    return pl.pallas_call(
        functools.partial(_gather_rows_kernel, tile=tile),
        out_shape=jax.ShapeDtypeStruct((n_out,) + src.shape[1:], src.dtype),
        grid=(n_out // tile,),
        in_specs=[pl.BlockSpec((1, 1, tile), lambda t: (t, 0, 0), memory_space=pltpu.SMEM),
                  pl.BlockSpec(memory_space=pl.ANY)],
        out_specs=pl.BlockSpec(memory_space=pl.ANY),
        scratch_shapes=[pltpu.SemaphoreType.DMA(())],
        compiler_params=_cp(("arbitrary",)),
        name="moe_gather",
    )(idx.reshape(n_out // tile, 1, tile), src)


def _moe_ffn_kernel(te_ref, nv_ref, x_ref, wg_ref, wu_ref, wd_ref, o_ref, acc):
    t = pl.program_id(0)
    f = pl.program_id(1)

    @pl.when(t < nv_ref[0])
    def _():
        y = _swiglu_acc(x_ref[...], wg_ref[0], wu_ref[0], wd_ref[0])

        @pl.when(f == 0)
        def _():
            acc[...] = y

        @pl.when(f > 0)
        def _():
            acc[...] += y

        @pl.when(f == pl.num_programs(1) - 1)
        def _():
            o_ref[...] = acc[...].astype(o_ref.dtype)

    @pl.when(jnp.logical_and(t >= nv_ref[0], f == pl.num_programs(1) - 1))
    def _():
        o_ref[...] = jnp.zeros(o_ref.shape, o_ref.dtype)


def moe_ffn(tile_expert, n_valid_tiles, xs, wg, wu, wd, *, tile, tf=512):
    ns, d = xs.shape
    dff = wg.shape[2]
    grid_spec = pltpu.PrefetchScalarGridSpec(
        num_scalar_prefetch=2, grid=(ns // tile, dff // tf),
        in_specs=[pl.BlockSpec((tile, d), lambda t, f, te, nv: (t, 0)),
                  pl.BlockSpec((1, d, tf), lambda t, f, te, nv: (te[t], 0, jnp.where(t < nv[0], f, dff // tf - 1))),
                  pl.BlockSpec((1, d, tf), lambda t, f, te, nv: (te[t], 0, jnp.where(t < nv[0], f, dff // tf - 1))),
                  pl.BlockSpec((1, tf, d), lambda t, f, te, nv: (te[t], jnp.where(t < nv[0], f, dff // tf - 1), 0))],
        out_specs=pl.BlockSpec((tile, d), lambda t, f, te, nv: (t, 0)),
        scratch_shapes=[pltpu.VMEM((tile, d), F32)])
    return pl.pallas_call(
        _moe_ffn_kernel,
        out_shape=jax.ShapeDtypeStruct((ns, d), BF16),
        grid_spec=grid_spec,
        compiler_params=_cp(("arbitrary", "arbitrary")),
        name="moe_ffn",
    )(tile_expert, n_valid_tiles, xs, wg, wu, wd)


def _moe_combine_kernel(s1_ref, s2_ref, x_ref, w_ref, y_hbm, o_ref, y1, y2, sem, *, tm):
    def issue(r, carry):
        pltpu.make_async_copy(y_hbm.at[pl.ds(s1_ref[0, 0, r], 1)], y1.at[pl.ds(r, 1)], sem.at[0]).start()
        pltpu.make_async_copy(y_hbm.at[pl.ds(s2_ref[0, 0, r], 1)], y2.at[pl.ds(r, 1)], sem.at[1]).start()
        return carry
    lax.fori_loop(0, tm, issue, 0)
    pltpu.make_async_copy(y_hbm.at[pl.ds(0, tm)], y1, sem.at[0]).wait()
    pltpu.make_async_copy(y_hbm.at[pl.ds(0, tm)], y2, sem.at[1]).wait()
    w = w_ref[...]
    o_ref[...] = x_ref[...] + w[0] * y1[...].astype(F32) + w[1] * y2[...].astype(F32)


def moe_combine(slot1, slot2, x3, w3, y3, *, tm=256):
    n = x3.shape[0]
    tm = min(tm, n)
    sspec = pl.BlockSpec((1, 1, tm), lambda i: (i, 0, 0), memory_space=pltpu.SMEM)
    slab = pl.BlockSpec((tm,) + x3.shape[1:], lambda i: (i, 0, 0))
    return pl.pallas_call(
        functools.partial(_moe_combine_kernel, tm=tm),
        out_shape=jax.ShapeDtypeStruct(x3.shape, F32),
        grid=(n // tm,),
        in_specs=[sspec, sspec, slab,
                  pl.BlockSpec((2, tm, 1, LANE), lambda i: (0, i, 0, 0)),
                  pl.BlockSpec(memory_space=pl.ANY)],
        out_specs=slab,
        scratch_shapes=[pltpu.VMEM((tm,) + y3.shape[1:], y3.dtype), pltpu.VMEM((tm,) + y3.shape[1:], y3.dtype),
                        pltpu.SemaphoreType.DMA((2,))],
        compiler_params=_cp(("arbitrary",)),
        name="moe_combine",
    )(slot1.reshape(n // tm, 1, tm), slot2.reshape(n // tm, 1, tm), x3, w3, y3)


def moe_mixer(xs_list, g, wr, wg, wu, wd):
    routed = [moe_route(x, g, wr) for x in xs_list]
    h = jnp.concatenate([r[0] for r in routed], axis=0)
    ti = jnp.concatenate([r[1][:, :2] for r in routed], axis=0)
    n = h.shape[0]
    tile = MOE_TILE
    e_flat = ti.reshape(-1)
    onehot = (e_flat[:, None] == jnp.arange(N_EXPERTS, dtype=jnp.int32)[None, :]).astype(jnp.int32)
    csum = jnp.cumsum(onehot, axis=0)
    rank = jnp.sum(csum * onehot, axis=1) - 1
    counts = csum[-1]
    gpad = ((counts + tile - 1) // tile) * tile
    gend = jnp.cumsum(gpad)
    slot = (gend - gpad)[e_flat] + rank
    n_slots = ((2 * n + N_EXPERTS * (tile - 1)) // tile) * tile
    n_tiles = n_slots // tile
    nv = (gend[-1] // tile).astype(jnp.int32).reshape(1)
    tok_of_slot = jnp.zeros((n_slots,), jnp.int32).at[slot].set(jnp.arange(2 * n, dtype=jnp.int32) // 2)
    tstart = jnp.arange(n_tiles, dtype=jnp.int32) * tile
    te = jnp.sum((tstart[:, None] >= gend[None, :]).astype(jnp.int32), axis=1)
    te = jnp.minimum(te, te[jnp.maximum(nv[0] - 1, 0)]).astype(jnp.int32)

    d = h.shape[1]
    slab = (d // LANE, LANE)
    xs_sorted = gather_rows(tok_of_slot, h.reshape((n,) + slab), tile=tile).reshape(n_slots, d)
    y3 = moe_ffn(te, nv, xs_sorted, wg, wu, wd, tile=tile).reshape((n_slots,) + slab)
    slot2 = slot.reshape(n, 2)
    outs, off = [], 0
    for x, r in zip(xs_list, routed):
        m = x.shape[0]
        w3 = jnp.broadcast_to(r[2][:, :2].T[:, :, None, None], (2, m, 1, LANE))
        o3 = moe_combine(slot2[off:off + m, 0], slot2[off:off + m, 1], x.reshape((m,) + slab), w3, y3)
        outs.append(o3.reshape(m, d))
        off += m
    return outs


def _rope_table(pos):
    half = MLA_ROPE // 2
    inv = ROPE_THETA ** (-jnp.arange(half, dtype=F32) / half)
    ang = pos.astype(F32)[:, None] * inv[None, :]
    c, s = jnp.cos(ang), jnp.sin(ang)
    return jnp.concatenate([c, c, -s, s], axis=-1)


def _swap_pair(g):
    half = MLA_ROPE // 2
    return jnp.concatenate([g, g[half:], g[:half]])[None, :]


def kernel(x_prompt, x_sample, cache_kv_latent, cache_k_rope, state_gla, state_conv, page_table, g_mix, w_in, b_in, gla_w_a2, gla_b_a, gla_g_out, conv_w, conv_b, conv_ln_g, conv_ln_b, mla_g_cq, mla_g_ckv, mla_g_kpe, mla_w_uq, mla_g_qn, mla_g_qr, mla_w_uk, mla_g_kn, mla_w_uv, w_branch, w_out, g_ffn, ffn_w_gate, ffn_w_up, ffn_w_down, moe_w_router, moe_w_gate, moe_w_up, moe_w_down):
    depth = w_in.shape[0]
    bp, tp, d = x_prompt.shape
    bs, ts, _ = x_sample.shape
    assert bp == 1
    past = page_table.shape[1] * cache_kv_latent.shape[2]
    H = MLA_HEADS
    half = MLA_ROPE // 2

    xp = x_prompt.reshape(bp * tp, d)
    xs = x_sample.reshape(bs * ts, d)
    tab_p = _rope_table(jnp.arange(tp, dtype=jnp.int32))
    tab_s = jnp.tile(_rope_table(past + jnp.arange(ts, dtype=jnp.int32)), (bs, 1))
    s0_p = jnp.zeros((bp, GLA_HEADS, GLA_DK, GLA_DV), F32)
    cache_kpeT = jnp.swapaxes(cache_k_rope, 2, 3)

    outs = [[] for _ in range(8)]
    for l in range(depth):
        w_in_r, b_in_r = _rearrange_in_proj(w_in[l], b_in[l])
        g_mix_l = g_mix[l][None, :]
        wa2p = jnp.pad(gla_w_a2[l], ((0, LANE - GLA_RANK), (0, 0)))
        ba = gla_b_a[l][None, :]
        gg = gla_g_out[l].reshape(1, GLA_HEADS * GLA_DV)
        cw, cb = conv_w[l], conv_b[l][None, :]
        lg, lb = conv_ln_g[l][None, :], conv_ln_b[l][None, :]
        gcq, gckv = mla_g_cq[l][None, :], mla_g_ckv[l][None, :]
        gkpe2, gqr2 = _swap_pair(mla_g_kpe[l]), _swap_pair(mla_g_qr[l])
        gqn, gkn = mla_g_qn[l][None, :], mla_g_kn[l][None, :]
        wq = mla_w_uq[l]
        wq_rope = wq[..., MLA_NOPE:]
        wuq = jnp.concatenate([wq, wq_rope[..., half:], wq_rope[..., :half]], axis=-1)
        wuq = wuq.transpose(1, 0, 2).astype(BF16)
        wk = mla_w_uk[l].transpose(1, 0, 2).astype(BF16)
        wkT = mla_w_uk[l].transpose(1, 2, 0).astype(BF16)
        wuv = mla_w_uv[l].transpose(1, 0, 2).astype(BF16)
        wuvT = mla_w_uv[l].transpose(1, 2, 0).astype(BF16)
        wbr = w_branch[l].astype(BF16)
        wo = w_out[l].astype(BF16)

        zp = in_proj(xp, g_mix_l, w_in_r, b_in_r)
        oa_p, sg_p = gla_branch(zp, wa2p, ba, gg, s0_p, batch=bp, seq=tp, chunk=64, tt=512, nb=1)
        ob_p, sc_p = conv_prompt(zp, cw, cb, lg, lb)
        ckv_p, kpe_p, qcT_p, kc_p, ckvT_p = mla_prep(zp, tab_p, gcq, gckv, gkpe2, wuq, gqn, gqr2, wk, gkn, sample=False)
        oc_p = flash_prompt(qcT_p, kc_p, ckvT_p, wuvT)
        xp = out_proj(branch_merge(oa_p, ob_p, oc_p, zp, wbr), wo, xp)

        zs = in_proj(xs, g_mix_l, w_in_r, b_in_r)
        oa_s, sg_s = gla_branch(zs, wa2p, ba, gg, state_gla[l], batch=bs, seq=ts, chunk=16, tt=ts, nb=8)
        bufp = jnp.pad(state_conv[l], ((0, 0), (2, 0), (0, 0)))
        ob_s, sc_s = conv_sample(zs, bufp, cw, cb, lg, lb, batch=bs, ts=ts)
        ckv_s, kpe_s, qa_s, qp_s = mla_prep(zs, tab_s, gcq, gckv, gkpe2, wuq, gqn, gqr2, wkT, gkn, sample=True)
        qa_s = qa_s.reshape(bs, ts, H, 256).transpose(0, 2, 1, 3).reshape(bs, H * ts, 256)
        qp_s = qp_s.reshape(bs, ts, H, LANE).transpose(0, 2, 1, 3).reshape(bs, H * ts, LANE)
        kpe_sT = jnp.pad(kpe_s.reshape(bs, ts, MLA_ROPE).transpose(0, 2, 1), ((0, 0), (0, 0), (0, LANE - ts)))
        oc_s = attend_sample(page_table, cache_kv_latent, cache_kpeT, l, qa_s, qp_s, ckv_s, kpe_sT,
                             wkT.reshape(H * MLA_NOPE, 256), wuv, batch=bs, ts=ts)
        xs = out_proj(branch_merge(oa_s, ob_s, oc_s, zs, wbr), wo, xs)

        gf = g_ffn[l][None, :]
        if l % 2 == 0:
            wg_, wu_, wd_ = (w[l // 2].astype(BF16) for w in (ffn_w_gate, ffn_w_up, ffn_w_down))
            xp = ffn_dense(xp, gf, wg_, wu_, wd_)
            xs = ffn_dense(xs, gf, wg_, wu_, wd_)
        else:
            wr = jnp.pad(moe_w_router[l // 2], ((0, 0), (0, LANE - N_EXPERTS)))
            wg_, wu_, wd_ = (w[l // 2].astype(BF16) for w in (moe_w_gate, moe_w_up, moe_w_down))
            xp, xs = moe_mixer([xp, xs], gf, wr, wg_, wu_, wd_)

        for k, v in enumerate((ckv_p.reshape(bp, tp, -1), kpe_p.reshape(bp, tp, -1), sg_p, sc_p,
                               ckv_s.reshape(bs, ts, -1), kpe_s.reshape(bs, ts, -1), sg_s, sc_s)):
            outs[k].append(v)

    return (xp.reshape(bp, tp, d), xs.reshape(bs, ts, d)) + tuple(jnp.stack(o) for o in outs)
```

```python
import functools
import math

import numpy as np
import jax
import jax.numpy as jnp
from jax import lax
from jax.experimental import pallas as pl
from jax.experimental.pallas import tpu as pltpu

F32 = jnp.float32
BF16 = jnp.bfloat16
EPS = 1e-6
HI = lax.Precision.HIGHEST

GLA_HEADS = 4
GLA_DK = 128
GLA_DV = 256
GLA_RANK = 16
GLA_TAU = 16.0
CONV_WIDTH = 31
HIST = CONV_WIDTH - 1
MLA_HEADS = 8
MLA_NOPE = 128
MLA_ROPE = 64
MLA_V = 128
MLA_SCALE = (MLA_NOPE + MLA_ROPE) ** -0.5
ROPE_THETA = 10000.0
N_EXPERTS = 8
LANE = 128
VMEM_LIMIT = 56 * 1024 * 1024


def _cp(sem, vmem=VMEM_LIMIT):
    return pltpu.CompilerParams(dimension_semantics=sem, vmem_limit_bytes=vmem)


def _sigmoid(x):
    return 1.0 / (1.0 + jnp.exp(-x))


def _log_sigmoid(x):
    return jnp.minimum(x, 0.0) - jnp.log(1.0 + jnp.exp(-jnp.abs(x)))


def _rms(x, g):
    return x * lax.rsqrt(jnp.mean(x * x, axis=-1, keepdims=True) + EPS) * g


def _dot(a, b, **kw):
    return jnp.dot(a, b, preferred_element_type=F32, **kw)


def _dot_nt(a, b, **kw):
    return lax.dot_general(a, b, (((1,), (1,)), ((), ())), preferred_element_type=F32, **kw)


def _dot_tn(a, b, **kw):
    return lax.dot_general(a, b, (((0,), (0,)), ((), ())), preferred_element_type=F32, **kw)


D_MODEL = 2048
BRANCH_W = 1024
Z_GATES = 0
Z_QA = 3 * D_MODEL
Z_KA = Z_QA + 512
Z_VA = Z_KA + 512
Z_RA = Z_VA + 1024
Z_GU = Z_RA + 1024
Z_GG = Z_GU + 1024
Z_CQ = Z_GG + 1024
Z_CKV = Z_CQ + 512
Z_KR = Z_CKV + 256
Z_ALR = Z_KR + 128
Z_DIM = Z_ALR + 128


def _rearrange_in_proj(w, b):
    def cols(a, s, n):
        return a[..., s:s + n]
    half = MLA_ROPE // 2
    def build(a):
        kr = cols(a, 5904, 64)
        pad = jnp.zeros(a.shape[:-1] + (LANE - GLA_RANK,), a.dtype)
        return jnp.concatenate([
            cols(a, 5968, 6144),
            cols(a, 0, 512), cols(a, 512, 512),
            cols(a, 1024, 1024), cols(a, 2048, 1024),
            cols(a, 3088, 1024), cols(a, 4112, 1024),
            cols(a, 5136, 512), cols(a, 5648, 256),
            kr, kr[..., half:], kr[..., :half],
            cols(a, 3072, GLA_RANK), pad], axis=-1)
    return build(w).astype(BF16), build(b)[None, :]


def _in_proj_kernel(x_ref, g_ref, w_ref, b_ref, o_ref, h_ref):
    @pl.when(pl.program_id(1) == 0)
    def _():
        h_ref[...] = _rms(x_ref[...], g_ref[...]).astype(BF16)
    o_ref[...] = _dot(h_ref[...], w_ref[...]) + b_ref[...]


def in_proj(x, g, w, b, *, tm=1024, tn=512):
    n, d = x.shape
    zd = w.shape[1]
    tm = min(tm, n)
    return pl.pallas_call(
        _in_proj_kernel,
        out_shape=jax.ShapeDtypeStruct((n, zd), F32),
        grid=(n // tm, zd // tn),
        in_specs=[pl.BlockSpec((tm, d), lambda i, j: (i, 0)),
                  pl.BlockSpec((1, d), lambda i, j: (0, 0)),
                  pl.BlockSpec((d, tn), lambda i, j: (0, j)),
                  pl.BlockSpec((1, tn), lambda i, j: (0, j))],
        out_specs=pl.BlockSpec((tm, tn), lambda i, j: (i, j)),
        scratch_shapes=[pltpu.VMEM((tm, d), BF16)],
        compiler_params=_cp(("parallel", "arbitrary")),
        name="in_proj",
    )(x, g, w, b)


def _gla_kernel(q_ref, k_ref, v_ref, r_ref, alr_ref, wa2_ref, ba_ref, gg_ref, s0_ref,
                o_ref, sout_ref, s_scr, *, seg, n_seg, chain):
    t = pl.program_id(2)
    C, G = seg, n_seg
    T = C * G
    sh = C.bit_length() - 1

    if chain:
        @pl.when(t == 0)
        def _():
            s_scr[...] = s0_ref[0, 0]

    row = lax.broadcasted_iota(jnp.int32, (T, T), 0)
    col = lax.broadcasted_iota(jnp.int32, (T, T), 1)
    same = lax.shift_right_logical(row, sh) == lax.shift_right_logical(col, sh)
    tril = jnp.logical_and(same, row >= col)
    la = _log_sigmoid(_dot(alr_ref[...], wa2_ref[...], precision=HI) + ba_ref[...]) * (1.0 / GLA_TAU)
    hi = la.astype(BF16)
    r1 = la - hi.astype(F32)
    mid = r1.astype(BF16)
    la3 = jnp.concatenate([hi, mid, (r1 - mid.astype(F32)).astype(BF16)], axis=1)
    fold = lambda x: x[:, 0:GLA_DK] + x[:, GLA_DK:2 * GLA_DK] + x[:, 2 * GLA_DK:3 * GLA_DK]
    ind = lambda m: jnp.where(m, 1.0, 0.0).astype(BF16)
    b = fold(_dot(ind(tril), la3))
    btot = fold(_dot(ind(same), la3))
    k = k_ref[...]
    v = v_ref[...].astype(BF16)
    qd32 = q_ref[...] * (GLA_DK ** -0.5) * jnp.exp(b)
    kd32 = k * jnp.exp(btot - b)
    qd = qd32.astype(BF16)
    ki = (k * jnp.exp(-b)).astype(BF16)
    att = jnp.where(tril, _dot_nt(qd, ki), 0.0).astype(BF16)
    o_intra = _dot(att, v)
    seg_of_row = lax.shift_right_logical(lax.broadcasted_iota(jnp.int32, (T, LANE), 0), sh)
    onehot = ind(seg_of_row == lax.broadcasted_iota(jnp.int32, (T, LANE), 1))
    bc3 = _dot_tn(la3, onehot)
    dec_all = jnp.exp(bc3[0:GLA_DK] + bc3[GLA_DK:2 * GLA_DK] + bc3[2 * GLA_DK:3 * GLA_DK])
    row1 = lax.shift_right_logical(lax.broadcasted_iota(jnp.int32, (T, 1), 0), sh)

    S = s_scr[...] if chain else None
    o_inter = []
    for g in range(G):
        rows = slice(g * C, (g + 1) * C)
        if not chain:
            S = s0_ref[g, 0]
        o_inter.append(_dot(qd32[rows].astype(BF16), S.astype(BF16)))
        if C % 16 == 0:
            kv = _dot_tn(kd32[rows].astype(BF16), v[rows])
        else:
            kv = _dot_tn(jnp.where(row1 == g, kd32, 0.0).astype(BF16), v)
        S = jnp.broadcast_to(dec_all[:, g:g + 1], S.shape) * S + kv
        if not chain:
            sout_ref[g, 0] = S
    o = o_intra + jnp.concatenate(o_inter, axis=0)
    rg = r_ref[...]
    o_ref[...] = (_rms(o, gg_ref[...]) * (rg * _sigmoid(rg))).astype(o_ref.dtype)

    if chain:
        s_scr[...] = S

        @pl.when(t == pl.num_programs(2) - 1)
        def _():
            sout_ref[0, 0] = S


def gla_branch(z, wa2p, ba, gg, s0, *, layer, batch, seq, chunk, tt, nb):
    n = z.shape[0]
    if seq < chunk:
        nb = min(nb, batch)
        seg, n_seg, chain, nt, rows = seq, nb, False, 1, nb * seq
    else:
        seg, n_seg, chain, nt, rows = chunk, tt // chunk, True, seq // tt, tt
        assert nb == 1 and batch == 1
    kern = functools.partial(_gla_kernel, seg=seg, n_seg=n_seg, chain=chain)
    qo, ko, vo, ro, ao = Z_QA // 128, Z_KA // 128, Z_VA // 256, Z_RA // 256, Z_ALR // 128
    rowmap = lambda b, h, t: b * nt + t
    return pl.pallas_call(
        kern,
        out_shape=(jax.ShapeDtypeStruct((n, GLA_HEADS * GLA_DV), BF16),
                   jax.ShapeDtypeStruct((batch, GLA_HEADS, GLA_DK, GLA_DV), F32)),
        grid=(batch // nb, GLA_HEADS, nt),
        in_specs=[pl.BlockSpec((rows, 128), lambda b, h, t: (rowmap(b, h, t), qo + h)),
                  pl.BlockSpec((rows, 128), lambda b, h, t: (rowmap(b, h, t), ko + h)),
                  pl.BlockSpec((rows, 256), lambda b, h, t: (rowmap(b, h, t), vo + h)),
                  pl.BlockSpec((rows, 256), lambda b, h, t: (rowmap(b, h, t), ro + h)),
                  pl.BlockSpec((rows, 128), lambda b, h, t: (rowmap(b, h, t), ao)),
                  pl.BlockSpec((128, 128), lambda b, h, t: (0, h)),
                  pl.BlockSpec((1, 128), lambda b, h, t: (0, h)),
                  pl.BlockSpec((1, 256), lambda b, h, t: (0, h)),
                  pl.BlockSpec((None, nb, 1, GLA_DK, GLA_DV), lambda b, h, t: (layer, b, h, 0, 0))],
        out_specs=(pl.BlockSpec((rows, 256), lambda b, h, t: (rowmap(b, h, t), h)),
                   pl.BlockSpec((nb, 1, GLA_DK, GLA_DV), lambda b, h, t: (b, h, 0, 0))),
        scratch_shapes=[pltpu.VMEM((GLA_DK, GLA_DV), F32)],
        compiler_params=_cp(("parallel", "parallel", "arbitrary")),
        name="gla",
    )(z, z, z, z, z, wa2p, ba, gg, s0)


def _ln_swish(y, lg, lb):
    mu = jnp.mean(y, axis=-1, keepdims=True)
    yc = y - mu
    var = jnp.mean(yc * yc, axis=-1, keepdims=True)
    yn = yc * lax.rsqrt(var + EPS) * lg + lb
    return yn * _sigmoid(yn)


def _conv_prompt_kernel(u_ref, g_ref, w_ref, cb_ref, lg_ref, lb_ref, o_ref, st_ref, win, ybuf, *, tt):
    i = pl.program_id(0)
    ch = u_ref.shape[1]

    @pl.when(i == 0)
    def _():
        win[0:32, :] = jnp.zeros((32, ch), F32)

    g = g_ref[...]
    win[32:32 + tt, :] = u_ref[...] * _sigmoid(g)
    for c0 in range(0, ch, LANE):
        cs = slice(c0, c0 + LANE)
        acc = jnp.zeros((tt, LANE), F32)
        for j in range(CONV_WIDTH):
            acc = acc + w_ref[j:j + 1, cs] * win[2 + j:2 + j + tt, cs]
        ybuf[:, cs] = acc + cb_ref[:, cs]
    o_ref[...] = _ln_swish(ybuf[...], lg_ref[...], lb_ref[...]).astype(o_ref.dtype)

    @pl.when(i == pl.num_programs(0) - 1)
    def _():
        st_ref[0] = win[tt + 2:tt + 32, :]

    win[0:32, :] = win[tt:tt + 32, :]


def conv_prompt(z, w, cb, lg, lb, *, tt=256):
    n = z.shape[0]
    ch = BRANCH_W
    uo, go = Z_GU // ch, Z_GG // ch
    return pl.pallas_call(
        functools.partial(_conv_prompt_kernel, tt=tt),
        out_shape=(jax.ShapeDtypeStruct((n, ch), BF16),
                   jax.ShapeDtypeStruct((1, HIST, ch), F32)),
        grid=(n // tt,),
        in_specs=[pl.BlockSpec((tt, ch), lambda i: (i, uo)),
                  pl.BlockSpec((tt, ch), lambda i: (i, go)),
                  pl.BlockSpec((CONV_WIDTH, ch), lambda i: (0, 0)),
                  pl.BlockSpec((1, ch), lambda i: (0, 0)),
                  pl.BlockSpec((1, ch), lambda i: (0, 0)),
                  pl.BlockSpec((1, ch), lambda i: (0, 0))],
        out_specs=(pl.BlockSpec((tt, ch), lambda i: (i, 0)),
                   pl.BlockSpec((1, HIST, ch), lambda i: (0, 0, 0))),
        scratch_shapes=[pltpu.VMEM((tt + 32, ch), F32), pltpu.VMEM((tt, ch), F32)],
        compiler_params=_cp(("arbitrary",)),
        name="conv_prompt",
    )(z, z, w, cb, lg, lb)


def _conv_sample_kernel(u_ref, g_ref, buf_ref, w_ref, cb_ref, lg_ref, lb_ref, o_ref, st_ref, win, *, nb, ts):
    ch = u_ref.shape[1]
    g = g_ref[...]
    a = u_ref[...] * _sigmoid(g)
    for i in range(nb):
        win[i, 0:32, :] = buf_ref[i]
        win[i, 32:32 + ts, :] = a[i * ts:(i + 1) * ts]
        acc = jnp.zeros((ts, ch), F32)
        for j in range(CONV_WIDTH):
            acc = acc + w_ref[j:j + 1, :] * win[i, 2 + j:2 + j + ts, :]
        y = acc + cb_ref[...]
        o_ref[i * ts:(i + 1) * ts, :] = _ln_swish(y, lg_ref[...], lb_ref[...]).astype(o_ref.dtype)
        st_ref[i] = win[i, 2 + ts:32 + ts, :]


def conv_sample(z, bufp, w, cb, lg, lb, *, batch, ts, nb=8):
    n = z.shape[0]
    ch = BRANCH_W
    uo, go = Z_GU // ch, Z_GG // ch
    return pl.pallas_call(
        functools.partial(_conv_sample_kernel, nb=nb, ts=ts),
        out_shape=(jax.ShapeDtypeStruct((n, ch), BF16),
                   jax.ShapeDtypeStruct((batch, HIST, ch), F32)),
        grid=(batch // nb,),
        in_specs=[pl.BlockSpec((nb * ts, ch), lambda i: (i, uo)),
                  pl.BlockSpec((nb * ts, ch), lambda i: (i, go)),
                  pl.BlockSpec((nb, 32, ch), lambda i: (i, 0, 0)),
                  pl.BlockSpec((CONV_WIDTH, ch), lambda i: (0, 0)),
                  pl.BlockSpec((1, ch), lambda i: (0, 0)),
                  pl.BlockSpec((1, ch), lambda i: (0, 0)),
                  pl.BlockSpec((1, ch), lambda i: (0, 0))],
        out_specs=(pl.BlockSpec((nb * ts, ch), lambda i: (i, 0)),
                   pl.BlockSpec((nb, HIST, ch), lambda i: (i, 0, 0))),
        scratch_shapes=[pltpu.VMEM((nb, 32 + ts, ch), F32)],
        compiler_params=_cp(("parallel",)),
        name="conv_sample",
    )(z, z, bufp, w, cb, lg, lb)


def _rope_pair(v, gpair, tab):
    lane = lax.broadcasted_iota(jnp.int32, v.shape, 1)
    ms = jnp.sum(jnp.where(lane < MLA_ROPE, v * v, 0.0), axis=-1, keepdims=True) * (1.0 / MLA_ROPE)
    t = v * lax.rsqrt(ms + EPS) * gpair * tab
    return t + pltpu.roll(t, MLA_ROPE, axis=1)


def _mla_prep_kernel(cq_ref, ckv_ref, krp_ref, tab_ref, gcq_ref, gckv_ref, gkpe_ref, wuq_ref, gqn_ref, gqr_ref,
                     wk_ref, gkn_ref, *out_refs, sample):
    if sample:
        ckv_o, kpe_o, qa_o, qp_o = out_refs
    else:
        ckv_o, kpe_o, qa_o, kc_o, ckvb_o = out_refs
    tab = tab_ref[...]
    lane = lax.broadcasted_iota(jnp.int32, tab.shape, 1)
    ckv = _rms(ckv_ref[...], gckv_ref[...])
    ckv_o[...] = ckv
    kpe = _rope_pair(krp_ref[...], gkpe_ref[...], tab)
    kpe_o[...] = kpe[:, :MLA_ROPE]
    kpe_z = jnp.where(lane < MLA_ROPE, kpe, 0.0)
    cqn = _rms(cq_ref[...], gcq_ref[...]).astype(BF16)
    ckv_b = ckv.astype(BF16)
    if not sample:
        ckvb_o[...] = ckv.T.astype(BF16)
    for h in range(MLA_HEADS):
        qh = _dot(cqn, wuq_ref[h])
        qn = _rms(qh[:, :MLA_NOPE], gqn_ref[...])
        qp = _rope_pair(qh[:, MLA_NOPE:], gqr_ref[...], tab)
        qp_z = jnp.where(lane < MLA_ROPE, qp, 0.0) * MLA_SCALE
        if sample:
            qg = (qn * gkn_ref[...] * MLA_SCALE).astype(BF16)
            qa_o[:, h * 256:(h + 1) * 256] = _dot(qg, wk_ref[h]).astype(BF16)
            qp_o[:, h * LANE:(h + 1) * LANE] = qp_z.astype(BF16)
        else:
            qa_o[h, 0:LANE, :] = (qn * MLA_SCALE).T.astype(BF16)
            qa_o[h, LANE:2 * LANE, :] = qp_z.T.astype(BF16)
            kn = _rms(_dot(ckv_b, wk_ref[h]), gkn_ref[...])
            kc_o[h, :, 0:LANE] = kn.astype(BF16)
            kc_o[h, :, LANE:2 * LANE] = kpe_z.astype(BF16)


def mla_prep(z, tab, gcq, gckv, gkpe2, wuq, gqn, gqr2, wk, gkn, *, sample, tm=256):
    n = z.shape[0]
    tm = min(tm, n)
    H = MLA_HEADS
    full = lambda *s: pl.BlockSpec(s, lambda i: (0,) * len(s))
    in_specs = [pl.BlockSpec((tm, 512), lambda i: (i, Z_CQ // 512)),
                pl.BlockSpec((tm, 256), lambda i: (i, Z_CKV // 256)),
                pl.BlockSpec((tm, 128), lambda i: (i, Z_KR // 128)),
                pl.BlockSpec((tm, 128), lambda i: (i, 0)),
                full(1, 512), full(1, 256), full(1, 128),
                full(H, 512, 256), full(1, 128), full(1, 128),
                full(*wk.shape), full(1, 128)]
    if sample:
        out_shape = (jax.ShapeDtypeStruct((n, 256), F32), jax.ShapeDtypeStruct((n, MLA_ROPE), F32),
                     jax.ShapeDtypeStruct((n, H * 256), BF16), jax.ShapeDtypeStruct((n, H * LANE), BF16))
        out_specs = (pl.BlockSpec((tm, 256), lambda i: (i, 0)), pl.BlockSpec((tm, MLA_ROPE), lambda i: (i, 0)),
                     pl.BlockSpec((tm, H * 256), lambda i: (i, 0)), pl.BlockSpec((tm, H * LANE), lambda i: (i, 0)))
    else:
        out_shape = (jax.ShapeDtypeStruct((n, 256), F32), jax.ShapeDtypeStruct((n, MLA_ROPE), F32),
                     jax.ShapeDtypeStruct((H, 256, n), BF16), jax.ShapeDtypeStruct((H, n, 256), BF16),
                     jax.ShapeDtypeStruct((256, n), BF16))
        out_specs = (pl.BlockSpec((tm, 256), lambda i: (i, 0)), pl.BlockSpec((tm, MLA_ROPE), lambda i: (i, 0)),
                     pl.BlockSpec((H, 256, tm), lambda i: (0, 0, i)), pl.BlockSpec((H, tm, 256), lambda i: (0, i, 0)),
                     pl.BlockSpec((256, tm), lambda i: (0, i)))
    return pl.pallas_call(
        functools.partial(_mla_prep_kernel, sample=sample),
        out_shape=out_shape, grid=(n // tm,), in_specs=in_specs, out_specs=out_specs,
        compiler_params=_cp(("parallel",)),
        name="mla_prep_sample" if sample else "mla_prep_prompt",
    )(z, z, z, tab, gcq, gckv, gkpe2, wuq, gqn, gqr2, wk, gkn)


def _flash_kernel(qT_ref, k_ref, vT_ref, wuvT_ref, o_ref, m_scr, l_scr, acc_scr, *, tq):
    i = pl.program_id(0)
    j = pl.program_id(1)
    H = qT_ref.shape[0]

    @pl.when(j == 0)
    def _():
        m_scr[...] = jnp.full(m_scr.shape, -jnp.inf, F32)
        l_scr[...] = jnp.zeros(l_scr.shape, F32)
        acc_scr[...] = jnp.zeros(acc_scr.shape, F32)

    def update(masked):
        vT = vT_ref[...]
        krow = lax.broadcasted_iota(jnp.int32, (tq, tq), 0)
        qcol = lax.broadcasted_iota(jnp.int32, (tq, tq), 1)
        for h in range(H):
            sT = _dot(k_ref[h], qT_ref[h])
            if masked:
                sT = jnp.where(krow <= qcol, sT, -jnp.inf)
            m_prev = m_scr[h]
            m_new = jnp.maximum(m_prev, jnp.max(sT, axis=0, keepdims=True))
            alpha = jnp.exp(m_prev - m_new)
            pT = jnp.exp(sT - m_new)
            l_scr[h] = alpha * l_scr[h] + jnp.sum(pT, axis=0, keepdims=True)
            acc_scr[h] = alpha * acc_scr[h] + _dot(vT, pT.astype(BF16))
            m_scr[h] = m_new

    @pl.when(j < i)
    def _():
        update(False)

    @pl.when(j == i)
    def _():
        update(True)
        for h in range(H):
            o_latT = (acc_scr[h] / l_scr[h]).astype(BF16)
            oT = _dot(wuvT_ref[h], o_latT)
            o_ref[:, h * MLA_V:(h + 1) * MLA_V] = oT.T.astype(o_ref.dtype)


def flash_prompt(qcT, kc, ckvT_b, wuvT, *, tq=1024):
    H, dk, n = qcT.shape
    tq = min(tq, n)
    nq = n // tq
    return pl.pallas_call(
        functools.partial(_flash_kernel, tq=tq),
        out_shape=jax.ShapeDtypeStruct((n, H * MLA_V), BF16),
        grid=(nq, nq),
        in_specs=[pl.BlockSpec((H, dk, tq), lambda i, j: (0, 0, i)),
                  pl.BlockSpec((H, tq, dk), lambda i, j: (0, jnp.minimum(i, j), 0)),
                  pl.BlockSpec((256, tq), lambda i, j: (0, jnp.minimum(i, j))),
                  pl.BlockSpec((H, MLA_V, 256), lambda i, j: (0, 0, 0))],
        out_specs=pl.BlockSpec((tq, H * MLA_V), lambda i, j: (i, 0)),
        scratch_shapes=[pltpu.VMEM((H, 1, tq), F32), pltpu.VMEM((H, 1, tq), F32), pltpu.VMEM((H, 256, tq), F32)],
        compiler_params=_cp(("parallel", "arbitrary")),
        name="flash_prompt",
    )(qcT, kc, ckvT_b, wuvT)


ATTEND_PAGES_PER_STEP = 64
ATTEND_SUB_KEYS = 1024


def _attend_sample_kernel(pt_ref, ckv_hbm, kpeT_hbm, qa_ref, qp_ref, cnew_ref, pnewT_ref, wkt_ref, wuv_ref, o_ref,
                          cbuf, kbuf, lhs, m_scr, l_scr, acc_scr, sem, *, layer, ts, npg, n, page):
    b = pl.program_id(0)
    j = pl.program_id(1)
    nb = pl.num_programs(0)
    nj = pl.num_programs(1)
    step = b * nj + j
    slot = step % 2
    H = MLA_HEADS
    R = H * ts
    NK = H * MLA_NOPE
    tk = npg * page
    ppn = n // page

    def page_copies(bb, jj, sl):
        out = []
        for p in range(npg):
            pg = pt_ref[bb, jj * npg + p]
            out.append(pltpu.make_async_copy(ckv_hbm.at[layer, pg], cbuf.at[sl, p], sem.at[sl, 0]))
            out.append(pltpu.make_async_copy(kpeT_hbm.at[layer, pg], kbuf.at[sl, p], sem.at[sl, 1]))
        return out

    @pl.when(step == 0)
    def _():
        lhs[0:NK, :] = wkt_ref[...]
        for c in page_copies(0, 0, 0):
            c.start()

    @pl.when(step + 1 < nb * nj)
    def _():
        last_j = j + 1 == nj
        for c in page_copies(jnp.where(last_j, b + 1, b), jnp.where(last_j, 0, j + 1), 1 - slot):
            c.start()

    pltpu.make_async_copy(ckv_hbm.at[layer, pl.ds(0, npg)], cbuf.at[slot], sem.at[slot, 0]).wait()
    pltpu.make_async_copy(kpeT_hbm.at[layer, pl.ds(0, npg)], kbuf.at[slot], sem.at[slot, 1]).wait()

    @pl.when(j == 0)
    def _():
        lhs[NK:NK + R, :] = qa_ref[0]
        m_scr[...] = jnp.full(m_scr.shape, -jnp.inf, F32)
        l_scr[...] = jnp.zeros(l_scr.shape, F32)
        acc_scr[...] = jnp.zeros(acc_scr.shape, F32)

    qp = qp_ref[0][:, :MLA_ROPE]

    def scores(cb, kpT_b):
        m = cb.shape[0]
        big = _dot_nt(lhs[...], cb)
        kn = big[0:NK].reshape(H, MLA_NOPE, m)
        rinv = lax.rsqrt(jnp.sum(kn * kn, axis=1) * (1.0 / MLA_NOPE) + EPS)
        s = big[NK:NK + R].reshape(H, ts, m) * rinv[:, None, :]
        return s.reshape(R, m) + _dot(qp, kpT_b)

    def update(s, cb):
        m_prev = m_scr[...]
        m_new = jnp.maximum(m_prev, jnp.max(s, axis=-1, keepdims=True))
        alpha = jnp.exp(m_prev - m_new)
        p = jnp.exp(s - m_new)
        l_scr[...] = alpha * l_scr[...] + jnp.sum(p, axis=-1, keepdims=True)
        acc_scr[...] = alpha * acc_scr[...] + _dot(p.astype(BF16), cb)
        m_scr[...] = m_new

    prev = None
    for i in range(tk // n):
        cb = cbuf[slot, i * ppn:(i + 1) * ppn].reshape(n, 256).astype(BF16)
        kpb = jnp.concatenate([kbuf[slot, i * ppn + t] for t in range(ppn)], axis=1).astype(BF16)
        s = scores(cb, kpb)
        if prev is not None:
            update(*prev)
        prev = (s, cb)
    update(*prev)

    @pl.when(j == nj - 1)
    def _():
        c = jnp.concatenate([cnew_ref[...], jnp.zeros((LANE - ts, 256), F32)], axis=0).astype(BF16)
        key = lax.broadcasted_iota(jnp.int32, (R, LANE), 1)
        qi = lax.broadcasted_iota(jnp.int32, (R, LANE), 0) % ts
        s = jnp.where(key <= qi, scores(c, pnewT_ref[0].astype(BF16)), -jnp.inf)
        update(s, c)
        o_lat = acc_scr[...] / l_scr[...]
        for h in range(H):
            o_ref[:, h * MLA_V:(h + 1) * MLA_V] = _dot(o_lat[h * ts:(h + 1) * ts].astype(BF16), wuv_ref[h])


def attend_sample(page_table, cache_ckv, cache_kpeT, layer, qa, qp, ckv_new, kpe_newT, wkt, wuv, *, batch, ts):
    page = cache_ckv.shape[2]
    n_pages = page_table.shape[1]
    npg = math.gcd(ATTEND_PAGES_PER_STEP, n_pages)
    n = min(ATTEND_SUB_KEYS, npg * page)
    H = MLA_HEADS
    R = H * ts
    in_specs = [pl.BlockSpec(memory_space=pl.ANY), pl.BlockSpec(memory_space=pl.ANY),
                pl.BlockSpec((1, R, 256), lambda b, j, pt: (b, 0, 0)),
                pl.BlockSpec((1, R, LANE), lambda b, j, pt: (b, 0, 0)),
                pl.BlockSpec((ts, 256), lambda b, j, pt: (b, 0)),
                pl.BlockSpec((1, MLA_ROPE, LANE), lambda b, j, pt: (b, 0, 0)),
                pl.BlockSpec((H * MLA_NOPE, 256), lambda b, j, pt: (0, 0)),
                pl.BlockSpec((H, 256, MLA_V), lambda b, j, pt: (0, 0, 0))]
    grid_spec = pltpu.PrefetchScalarGridSpec(
        num_scalar_prefetch=1, grid=(batch, n_pages // npg), in_specs=in_specs,
        out_specs=pl.BlockSpec((ts, H * MLA_V), lambda b, j, pt: (b, 0)),
        scratch_shapes=[pltpu.VMEM((2, npg, page, 256), F32), pltpu.VMEM((2, npg, MLA_ROPE, page), F32),
                        pltpu.VMEM((H * MLA_NOPE + R, 256), BF16), pltpu.VMEM((R, 1), F32),
                        pltpu.VMEM((R, 1), F32), pltpu.VMEM((R, 256), F32), pltpu.SemaphoreType.DMA((2, 2))])
    return pl.pallas_call(
        functools.partial(_attend_sample_kernel, layer=layer, ts=ts, npg=npg, n=n, page=page),
        out_shape=jax.ShapeDtypeStruct((batch * ts, H * MLA_V), F32),
        grid_spec=grid_spec,
        compiler_params=_cp(("arbitrary", "arbitrary")),
        name="attend_sample",
    )(page_table, cache_ckv, cache_kpeT, qa, qp, ckv_new, kpe_newT, wkt, wuv)


def _merge_kernel(oa_ref, ob_ref, oc_ref, ga_ref, gb_ref, gc_ref, wa_ref, wb_ref, wc_ref, m_ref):
    acc = _sigmoid(ga_ref[...]) * _dot(oa_ref[...].astype(BF16), wa_ref[0])
    acc = acc + _sigmoid(gb_ref[...]) * _dot(ob_ref[...].astype(BF16), wb_ref[0])
    acc = acc + _sigmoid(gc_ref[...]) * _dot(oc_ref[...].astype(BF16), wc_ref[0])
    m_ref[...] = acc.astype(m_ref.dtype)


def branch_merge(oa, ob, oc, z, wbr, *, tm=512, tn=512):
    n = oa.shape[0]
    tm = min(tm, n)
    d = wbr.shape[2]
    w = oa.shape[1]
    nj = d // tn
    ospec = pl.BlockSpec((tm, w), lambda i, j: (i, 0))
    gspec = lambda k: pl.BlockSpec((tm, tn), lambda i, j: (i, k * nj + j))
    wspec = lambda k: pl.BlockSpec((1, w, tn), lambda i, j: (k, 0, j))
    return pl.pallas_call(
        _merge_kernel,
        out_shape=jax.ShapeDtypeStruct((n, d), BF16),
        grid=(n // tm, nj),
        in_specs=[ospec, ospec, ospec, gspec(0), gspec(1), gspec(2), wspec(0), wspec(1), wspec(2)],
        out_specs=pl.BlockSpec((tm, tn), lambda i, j: (i, j)),
        compiler_params=_cp(("parallel", "arbitrary")),
        name="branch_merge",
    )(oa, ob, oc, z, z, z, wbr, wbr, wbr)


def _out_proj_kernel(m_ref, w_ref, x_ref, o_ref):
    o_ref[...] = x_ref[...] + _dot(m_ref[...], w_ref[...])


def out_proj(m, w, x, *, tm=512, tn=512):
    n, d = x.shape
    tm = min(tm, n)
    return pl.pallas_call(
        _out_proj_kernel,
        out_shape=jax.ShapeDtypeStruct((n, d), F32),
        grid=(n // tm, d // tn),
        in_specs=[pl.BlockSpec((tm, d), lambda i, j: (i, 0)),
                  pl.BlockSpec((d, tn), lambda i, j: (0, j)),
                  pl.BlockSpec((tm, tn), lambda i, j: (i, j))],
        out_specs=pl.BlockSpec((tm, tn), lambda i, j: (i, j)),
        compiler_params=_cp(("parallel", "arbitrary")),
        name="out_proj",
    )(m, w, x)


def _swiglu_acc(h, wg, wu, wd):
    a = _dot(h, wg)
    u = _dot(h, wu)
    return _dot((a * _sigmoid(a) * u).astype(BF16), wd)


def _ffn_kernel(x_ref, g_ref, wg_ref, wu_ref, wd_ref, o_ref, h_ref):
    @pl.when(pl.program_id(1) == 0)
    def _():
        x = x_ref[...]
        h_ref[...] = _rms(x, g_ref[...]).astype(BF16)
        o_ref[...] = x
    o_ref[...] += _swiglu_acc(h_ref[...], wg_ref[...], wu_ref[...], wd_ref[...])


def ffn_dense(x, g, wg, wu, wd, *, tm=512, tf=512):
    n, d = x.shape
    dff = wg.shape[1]
    tm = min(tm, n)
    return pl.pallas_call(
        _ffn_kernel,
        out_shape=jax.ShapeDtypeStruct((n, d), F32),
        grid=(n // tm, dff // tf),
        in_specs=[pl.BlockSpec((tm, d), lambda i, f: (i, 0)),
                  pl.BlockSpec((1, d), lambda i, f: (0, 0)),
                  pl.BlockSpec((d, tf), lambda i, f: (0, f)),
                  pl.BlockSpec((d, tf), lambda i, f: (0, f)),
                  pl.BlockSpec((tf, d), lambda i, f: (f, 0))],
        out_specs=pl.BlockSpec((tm, d), lambda i, f: (i, 0)),
        scratch_shapes=[pltpu.VMEM((tm, d), BF16)],
        compiler_params=_cp(("parallel", "arbitrary")),
        name="ffn_dense",
    )(x, g, wg, wu, wd)


MOE_TILE = 512


def _route_kernel(x_ref, g_ref, wr_ref, h_ref, ti_ref, tw_ref):
    h = _rms(x_ref[...], g_ref[...])
    h_ref[...] = h.astype(BF16)
    logits = _dot(h, wr_ref[...], precision=HI)
    lane = lax.broadcasted_iota(jnp.int32, logits.shape, 1)
    logits = jnp.where(lane < N_EXPERTS, logits, -jnp.inf)
    m1 = jnp.max(logits, axis=-1, keepdims=True)
    i1 = jnp.min(jnp.where(logits == m1, lane, LANE), axis=-1, keepdims=True)
    rest = jnp.where(lane == i1, -jnp.inf, logits)
    m2 = jnp.max(rest, axis=-1, keepdims=True)
    i2 = jnp.min(jnp.where(rest == m2, lane, LANE), axis=-1, keepdims=True)
    e2 = jnp.exp(m2 - m1)
    w1 = 1.0 / (1.0 + e2)
    w2 = e2 / (1.0 + e2)
    ti_ref[...] = jnp.where(lane == 0, i1, jnp.where(lane == 1, i2, 0))
    tw_ref[...] = jnp.where(lane == 0, w1, jnp.where(lane == 1, w2, 0.0))


def moe_route(x, g, wr, *, tm=512):
    n, d = x.shape
    tm = min(tm, n)
    row = lambda w: pl.BlockSpec((tm, w), lambda i: (i, 0))
    return pl.pallas_call(
        _route_kernel,
        out_shape=(jax.ShapeDtypeStruct((n, d), BF16), jax.ShapeDtypeStruct((n, LANE), jnp.int32),
                   jax.ShapeDtypeStruct((n, LANE), F32)),
        grid=(n // tm,),
        in_specs=[row(d), pl.BlockSpec((1, d), lambda i: (0, 0)), pl.BlockSpec((d, LANE), lambda i: (0, 0))],
        out_specs=(row(d), row(LANE), row(LANE)),
        compiler_params=_cp(("parallel",)),
        name="moe_route",
    )(x, g, wr)


def _gather_rows_kernel(idx_ref, src_hbm, o_ref, sem, *, tile):
    def issue(r, carry):
        pltpu.make_async_copy(src_hbm.at[pl.ds(idx_ref[0, 0, r], 1)], o_ref.at[pl.ds(r, 1)], sem).start()
        return carry
    lax.fori_loop(0, tile, issue, 0, unroll=8)
    pltpu.make_async_copy(src_hbm.at[pl.ds(0, tile)], o_ref, sem).wait()


def gather_rows(idx, src, *, tile):
    n_out = idx.shape[0]
    return pl.pallas_call(
        functools.partial(_gather_rows_kernel, tile=tile),
        out_shape=jax.ShapeDtypeStruct((n_out,) + src.shape[1:], src.dtype),
        grid=(n_out // tile,),
        in_specs=[pl.BlockSpec((1, 1, tile), lambda t: (t, 0, 0), memory_space=pltpu.SMEM),
                  pl.BlockSpec(memory_space=pl.ANY)],
        out_specs=pl.BlockSpec((tile,) + src.shape[1:], lambda t: (t, 0, 0)),
        scratch_shapes=[pltpu.SemaphoreType.DMA(())],
        compiler_params=_cp(("arbitrary",)),
        name="moe_gather",
    )(idx.reshape(n_out // tile, 1, tile), src)


def _moe_ffn_kernel(te_ref, nv_ref, x_ref, wg_ref, wu_ref, wd_ref, o_ref, acc):
    t = pl.program_id(0)
    f = pl.program_id(1)

    @pl.when(t < nv_ref[0])
    def _():
        y = _swiglu_acc(x_ref[...], wg_ref[0], wu_ref[0], wd_ref[0])

        @pl.when(f == 0)
        def _():
            acc[...] = y

        @pl.when(f > 0)
        def _():
            acc[...] += y

        @pl.when(f == pl.num_programs(1) - 1)
        def _():
            o_ref[...] = acc[...].astype(o_ref.dtype)

    @pl.when(jnp.logical_and(t >= nv_ref[0], f == pl.num_programs(1) - 1))
    def _():
        o_ref[...] = jnp.zeros(o_ref.shape, o_ref.dtype)


def moe_ffn(tile_expert, n_valid_tiles, xs, wg, wu, wd, *, tile, tf=512):
    ns, d = xs.shape
    dff = wg.shape[2]
    grid_spec = pltpu.PrefetchScalarGridSpec(
        num_scalar_prefetch=2, grid=(ns // tile, dff // tf),
        in_specs=[pl.BlockSpec((tile, d), lambda t, f, te, nv: (t, 0)),
                  pl.BlockSpec((1, d, tf), lambda t, f, te, nv: (te[t], 0, jnp.where(t < nv[0], f, dff // tf - 1))),
                  pl.BlockSpec((1, d, tf), lambda t, f, te, nv: (te[t], 0, jnp.where(t < nv[0], f, dff // tf - 1))),
                  pl.BlockSpec((1, tf, d), lambda t, f, te, nv: (te[t], jnp.where(t < nv[0], f, dff // tf - 1), 0))],
        out_specs=pl.BlockSpec((tile, d), lambda t, f, te, nv: (t, 0)),
        scratch_shapes=[pltpu.VMEM((tile, d), F32)])
    return pl.pallas_call(
        _moe_ffn_kernel,
        out_shape=jax.ShapeDtypeStruct((ns, d), BF16),
        grid_spec=grid_spec,
        compiler_params=_cp(("arbitrary", "arbitrary")),
        name="moe_ffn",
    )(tile_expert, n_valid_tiles, xs, wg, wu, wd)


def _moe_combine_kernel(s1_ref, s2_ref, x_ref, w_ref, y_hbm, o_ref, y1, y2, sem, *, tm):
    def issue(r, carry):
        pltpu.make_async_copy(y_hbm.at[pl.ds(s1_ref[0, 0, r], 1)], y1.at[pl.ds(r, 1)], sem.at[0]).start()
        pltpu.make_async_copy(y_hbm.at[pl.ds(s2_ref[0, 0, r], 1)], y2.at[pl.ds(r, 1)], sem.at[1]).start()
        return carry
    lax.fori_loop(0, tm, issue, 0)
    pltpu.make_async_copy(y_hbm.at[pl.ds(0, tm)], y1, sem.at[0]).wait()
    pltpu.make_async_copy(y_hbm.at[pl.ds(0, tm)], y2, sem.at[1]).wait()
    w = w_ref[...]
    o_ref[...] = x_ref[...] + w[0] * y1[...].astype(F32) + w[1] * y2[...].astype(F32)


def moe_combine(slot1, slot2, x3, w3, y3, *, tm=256):
    n = x3.shape[0]
    tm = min(tm, n)
    sspec = pl.BlockSpec((1, 1, tm), lambda i: (i, 0, 0), memory_space=pltpu.SMEM)
    slab = pl.BlockSpec((tm,) + x3.shape[1:], lambda i: (i, 0, 0))
    return pl.pallas_call(
        functools.partial(_moe_combine_kernel, tm=tm),
        out_shape=jax.ShapeDtypeStruct(x3.shape, F32),
        grid=(n // tm,),
        in_specs=[sspec, sspec, slab,
                  pl.BlockSpec((2, tm, 1, LANE), lambda i: (0, i, 0, 0)),
                  pl.BlockSpec(memory_space=pl.ANY)],
        out_specs=slab,
        scratch_shapes=[pltpu.VMEM((tm,) + y3.shape[1:], y3.dtype), pltpu.VMEM((tm,) + y3.shape[1:], y3.dtype),
                        pltpu.SemaphoreType.DMA((2,))],
        compiler_params=_cp(("arbitrary",)),
        name="moe_combine",
    )(slot1.reshape(n // tm, 1, tm), slot2.reshape(n // tm, 1, tm), x3, w3, y3)


def moe_mixer(xs_list, g, wr, wg, wu, wd):
    routed = [moe_route(x, g, wr) for x in xs_list]
    h = jnp.concatenate([r[0] for r in routed], axis=0)
    ti = jnp.concatenate([r[1][:, :2] for r in routed], axis=0)
    n = h.shape[0]
    tile = MOE_TILE
    e_flat = ti.reshape(-1)
    onehot = (e_flat[:, None] == jnp.arange(N_EXPERTS, dtype=jnp.int32)[None, :]).astype(jnp.int32)
    csum = jnp.cumsum(onehot, axis=0)
    rank = jnp.sum(csum * onehot, axis=1) - 1
    counts = csum[-1]
    gpad = ((counts + tile - 1) // tile) * tile
    gend = jnp.cumsum(gpad)
    slot = (gend - gpad)[e_flat] + rank
    n_slots = ((2 * n + N_EXPERTS * (tile - 1)) // tile) * tile
    n_tiles = n_slots // tile
    nv = (gend[-1] // tile).astype(jnp.int32).reshape(1)
    tok_of_slot = jnp.zeros((n_slots,), jnp.int32).at[slot].set(jnp.arange(2 * n, dtype=jnp.int32) // 2)
    tstart = jnp.arange(n_tiles, dtype=jnp.int32) * tile
    te = jnp.sum((tstart[:, None] >= gend[None, :]).astype(jnp.int32), axis=1)
    te = jnp.minimum(te, te[jnp.maximum(nv[0] - 1, 0)]).astype(jnp.int32)

    d = h.shape[1]
    slab = (d // LANE, LANE)
    xs_sorted = gather_rows(tok_of_slot, h.reshape((n,) + slab), tile=tile).reshape(n_slots, d)
    y3 = moe_ffn(te, nv, xs_sorted, wg, wu, wd, tile=tile).reshape((n_slots,) + slab)
    slot2 = slot.reshape(n, 2)
    outs, off = [], 0
    for x, r in zip(xs_list, routed):
        m = x.shape[0]
        w3 = jnp.broadcast_to(r[2][:, :2].T[:, :, None, None], (2, m, 1, LANE))
        o3 = moe_combine(slot2[off:off + m, 0], slot2[off:off + m, 1], x.reshape((m,) + slab), w3, y3)
        outs.append(o3.reshape(m, d))
        off += m
    return outs


def _rope_table(pos):
    half = MLA_ROPE // 2
    inv = ROPE_THETA ** (-jnp.arange(half, dtype=F32) / half)
    ang = pos.astype(F32)[:, None] * inv[None, :]
    c, s = jnp.cos(ang), jnp.sin(ang)
    return jnp.concatenate([c, c, -s, s], axis=-1)


def _swap_pair(g):
    half = MLA_ROPE // 2
    return jnp.concatenate([g, g[half:], g[:half]])[None, :]


def kernel(x_prompt, x_sample, cache_kv_latent, cache_k_rope, state_gla, state_conv, page_table, g_mix, w_in, b_in, gla_w_a2, gla_b_a, gla_g_out, conv_w, conv_b, conv_ln_g, conv_ln_b, mla_g_cq, mla_g_ckv, mla_g_kpe, mla_w_uq, mla_g_qn, mla_g_qr, mla_w_uk, mla_g_kn, mla_w_uv, w_branch, w_out, g_ffn, ffn_w_gate, ffn_w_up, ffn_w_down, moe_w_router, moe_w_gate, moe_w_up, moe_w_down):
    depth = w_in.shape[0]
    bp, tp, d = x_prompt.shape
    bs, ts, _ = x_sample.shape
    assert bp == 1
    past = page_table.shape[1] * cache_kv_latent.shape[2]
    H = MLA_HEADS
    half = MLA_ROPE // 2

    xp = x_prompt.reshape(bp * tp, d)
    xs = x_sample.reshape(bs * ts, d)
    tab_p = _rope_table(jnp.arange(tp, dtype=jnp.int32))
    tab_s = jnp.tile(_rope_table(past + jnp.arange(ts, dtype=jnp.int32)), (bs, 1))
    s0_p = jnp.zeros((1, bp, GLA_HEADS, GLA_DK, GLA_DV), F32)
    cache_kpeT = jnp.swapaxes(cache_k_rope, 2, 3)

    outs = [[] for _ in range(8)]
    for l in range(depth):
        w_in_r, b_in_r = _rearrange_in_proj(w_in[l], b_in[l])
        g_mix_l = g_mix[l][None, :]
        wa2p = jnp.pad(gla_w_a2[l], ((0, LANE - GLA_RANK), (0, 0)))
        ba = gla_b_a[l][None, :]
        gg = gla_g_out[l].reshape(1, GLA_HEADS * GLA_DV)
        cw, cb = conv_w[l], conv_b[l][None, :]
        lg, lb = conv_ln_g[l][None, :], conv_ln_b[l][None, :]
        gcq, gckv = mla_g_cq[l][None, :], mla_g_ckv[l][None, :]
        gkpe2, gqr2 = _swap_pair(mla_g_kpe[l]), _swap_pair(mla_g_qr[l])
        gqn, gkn = mla_g_qn[l][None, :], mla_g_kn[l][None, :]
        wq = mla_w_uq[l]
        wq_rope = wq[..., MLA_NOPE:]
        wuq = jnp.concatenate([wq, wq_rope[..., half:], wq_rope[..., :half]], axis=-1)
        wuq = wuq.transpose(1, 0, 2).astype(BF16)
        wk = mla_w_uk[l].transpose(1, 0, 2).astype(BF16)
        wkT = mla_w_uk[l].transpose(1, 2, 0).astype(BF16)
        wuv = mla_w_uv[l].transpose(1, 0, 2).astype(BF16)
        wuvT = mla_w_uv[l].transpose(1, 2, 0).astype(BF16)
        wbr = w_branch[l].astype(BF16)
        wo = w_out[l].astype(BF16)

        zp = in_proj(xp, g_mix_l, w_in_r, b_in_r)
        oa_p, sg_p = gla_branch(zp, wa2p, ba, gg, s0_p, layer=0, batch=bp, seq=tp, chunk=64, tt=512, nb=1)
        ob_p, sc_p = conv_prompt(zp, cw, cb, lg, lb)
        ckv_p, kpe_p, qcT_p, kc_p, ckvT_p = mla_prep(zp, tab_p, gcq, gckv, gkpe2, wuq, gqn, gqr2, wk, gkn, sample=False)
        oc_p = flash_prompt(qcT_p, kc_p, ckvT_p, wuvT)
        xp = out_proj(branch_merge(oa_p, ob_p, oc_p, zp, wbr), wo, xp)

        zs = in_proj(xs, g_mix_l, w_in_r, b_in_r)
        oa_s, sg_s = gla_branch(zs, wa2p, ba, gg, state_gla, layer=l, batch=bs, seq=ts, chunk=16, tt=ts, nb=16)
        bufp = jnp.pad(state_conv[l], ((0, 0), (2, 0), (0, 0)))
        ob_s, sc_s = conv_sample(zs, bufp, cw, cb, lg, lb, batch=bs, ts=ts)
        ckv_s, kpe_s, qa_s, qp_s = mla_prep(zs, tab_s, gcq, gckv, gkpe2, wuq, gqn, gqr2, wkT, gkn, sample=True)
        qa_s = qa_s.reshape(bs, ts, H, 256).transpose(0, 2, 1, 3).reshape(bs, H * ts, 256)
        qp_s = qp_s.reshape(bs, ts, H, LANE).transpose(0, 2, 1, 3).reshape(bs, H * ts, LANE)
        kpe_sT = jnp.pad(kpe_s.reshape(bs, ts, MLA_ROPE).transpose(0, 2, 1), ((0, 0), (0, 0), (0, LANE - ts)))
        oc_s = attend_sample(page_table, cache_kv_latent, cache_kpeT, l, qa_s, qp_s, ckv_s, kpe_sT,
                             wkT.reshape(H * MLA_NOPE, 256), wuv, batch=bs, ts=ts)
        xs = out_proj(branch_merge(oa_s, ob_s, oc_s, zs, wbr), wo, xs)

        gf = g_ffn[l][None, :]
        if l % 2 == 0:
            wg_, wu_, wd_ = (w[l // 2].astype(BF16) for w in (ffn_w_gate, ffn_w_up, ffn_w_down))
            xp = ffn_dense(xp, gf, wg_, wu_, wd_)
            xs = ffn_dense(xs, gf, wg_, wu_, wd_)
        else:
            wr = jnp.pad(moe_w_router[l // 2], ((0, 0), (0, LANE - N_EXPERTS)))
            wg_, wu_, wd_ = (w[l // 2].astype(BF16) for w in (moe_w_gate, moe_w_up, moe_w_down))
            xp, xs = moe_mixer([xp, xs], gf, wr, wg_, wu_, wd_)

        for k, v in enumerate((ckv_p.reshape(bp, tp, -1), kpe_p.reshape(bp, tp, -1), sg_p, sc_p,
                               ckv_s.reshape(bs, ts, -1), kpe_s.reshape(bs, ts, -1), sg_s, sc_s)):
            outs[k].append(v)

    return (xp.reshape(bp, tp, d), xs.reshape(bs, ts, d)) + tuple(jnp.stack(o) for o in outs)
```

```python
import functools
import math

import numpy as np
import jax
import jax.numpy as jnp
from jax import lax
from jax.experimental import pallas as pl
from jax.experimental.pallas import tpu as pltpu

F32 = jnp.float32
BF16 = jnp.bfloat16
EPS = 1e-6
HI = lax.Precision.HIGHEST

GLA_HEADS = 4
GLA_DK = 128
GLA_DV = 256
GLA_RANK = 16
GLA_TAU = 16.0
CONV_WIDTH = 31
HIST = CONV_WIDTH - 1
MLA_HEADS = 8
MLA_NOPE = 128
MLA_ROPE = 64
MLA_V = 128
MLA_SCALE = (MLA_NOPE + MLA_ROPE) ** -0.5
ROPE_THETA = 10000.0
N_EXPERTS = 8
LANE = 128
VMEM_LIMIT = 56 * 1024 * 1024


def _cp(sem, vmem=VMEM_LIMIT):
    return pltpu.CompilerParams(dimension_semantics=sem, vmem_limit_bytes=vmem)


def _sigmoid(x):
    return 1.0 / (1.0 + jnp.exp(-x))


def _log_sigmoid(x):
    return jnp.minimum(x, 0.0) - jnp.log(1.0 + jnp.exp(-jnp.abs(x)))


def _rms(x, g):
    return x * lax.rsqrt(jnp.mean(x * x, axis=-1, keepdims=True) + EPS) * g


def _dot(a, b, **kw):
    return jnp.dot(a, b, preferred_element_type=F32, **kw)


def _dot_nt(a, b, **kw):
    return lax.dot_general(a, b, (((1,), (1,)), ((), ())), preferred_element_type=F32, **kw)


def _dot_tn(a, b, **kw):
    return lax.dot_general(a, b, (((0,), (0,)), ((), ())), preferred_element_type=F32, **kw)


D_MODEL = 2048
BRANCH_W = 1024
Z_GATES = 0
Z_QA = 3 * D_MODEL
Z_KA = Z_QA + 512
Z_VA = Z_KA + 512
Z_RA = Z_VA + 1024
Z_GU = Z_RA + 1024
Z_GG = Z_GU + 1024
Z_CQ = Z_GG + 1024
Z_CKV = Z_CQ + 512
Z_KR = Z_CKV + 256
Z_ALR = Z_KR + 128
Z_DIM = Z_ALR + 128


def _rearrange_in_proj(w, b):
    def cols(a, s, n):
        return a[..., s:s + n]
    half = MLA_ROPE // 2
    def build(a):
        kr = cols(a, 5904, 64)
        pad = jnp.zeros(a.shape[:-1] + (LANE - GLA_RANK,), a.dtype)
        return jnp.concatenate([
            cols(a, 5968, 6144),
            cols(a, 0, 512), cols(a, 512, 512),
            cols(a, 1024, 1024), cols(a, 2048, 1024),
            cols(a, 3088, 1024), cols(a, 4112, 1024),
            cols(a, 5136, 512), cols(a, 5648, 256),
            kr, kr[..., half:], kr[..., :half],
            cols(a, 3072, GLA_RANK), pad], axis=-1)
    return build(w).astype(BF16), build(b)[None, :]


def _in_proj_kernel(x_ref, g_ref, w_ref, b_ref, o_ref, h_ref):
    @pl.when(pl.program_id(1) == 0)
    def _():
        h_ref[...] = _rms(x_ref[...], g_ref[...]).astype(BF16)
    o_ref[...] = _dot(h_ref[...], w_ref[...]) + b_ref[...]


def in_proj(x, g, w, b, *, tm=1024, tn=512):
    n, d = x.shape
    zd = w.shape[1]
    tm = min(tm, n)
    return pl.pallas_call(
        _in_proj_kernel,
        out_shape=jax.ShapeDtypeStruct((n, zd), F32),
        grid=(n // tm, zd // tn),
        in_specs=[pl.BlockSpec((tm, d), lambda i, j: (i, 0)),
                  pl.BlockSpec((1, d), lambda i, j: (0, 0)),
                  pl.BlockSpec((d, tn), lambda i, j: (0, j)),
                  pl.BlockSpec((1, tn), lambda i, j: (0, j))],
        out_specs=pl.BlockSpec((tm, tn), lambda i, j: (i, j)),
        scratch_shapes=[pltpu.VMEM((tm, d), BF16)],
        compiler_params=_cp(("parallel", "arbitrary")),
        name="in_proj",
    )(x, g, w, b)


def _gla_kernel(q_ref, k_ref, v_ref, r_ref, alr_ref, wa2_ref, ba_ref, gg_ref, s0_ref,
                o_ref, sout_ref, s_scr, *, seg, n_seg, chain):
    t = pl.program_id(2)
    C, G = seg, n_seg
    T = C * G
    sh = C.bit_length() - 1

    if chain:
        @pl.when(t == 0)
        def _():
            s_scr[...] = s0_ref[0, 0]

    row = lax.broadcasted_iota(jnp.int32, (T, T), 0)
    col = lax.broadcasted_iota(jnp.int32, (T, T), 1)
    same = lax.shift_right_logical(row, sh) == lax.shift_right_logical(col, sh)
    tril = jnp.logical_and(same, row >= col)
    la = _log_sigmoid(_dot(alr_ref[...], wa2_ref[...], precision=HI) + ba_ref[...]) * (1.0 / GLA_TAU)
    hi = la.astype(BF16)
    r1 = la - hi.astype(F32)
    mid = r1.astype(BF16)
    la3 = jnp.concatenate([hi, mid, (r1 - mid.astype(F32)).astype(BF16)], axis=1)
    fold = lambda x: x[:, 0:GLA_DK] + x[:, GLA_DK:2 * GLA_DK] + x[:, 2 * GLA_DK:3 * GLA_DK]
    ind = lambda m: jnp.where(m, 1.0, 0.0).astype(BF16)
    b = fold(_dot(ind(tril), la3))
    btot = fold(_dot(ind(same), la3))
    k = k_ref[...]
    v = v_ref[...].astype(BF16)
    qd32 = q_ref[...] * (GLA_DK ** -0.5) * jnp.exp(b)
    kd32 = k * jnp.exp(btot - b)
    qd = qd32.astype(BF16)
    ki = (k * jnp.exp(-b)).astype(BF16)
    att = jnp.where(tril, _dot_nt(qd, ki), 0.0).astype(BF16)
    o_intra = _dot(att, v)
    seg_of_row = lax.shift_right_logical(lax.broadcasted_iota(jnp.int32, (T, LANE), 0), sh)
    onehot = ind(seg_of_row == lax.broadcasted_iota(jnp.int32, (T, LANE), 1))
    bc3 = _dot_tn(la3, onehot)
    dec_all = jnp.exp(bc3[0:GLA_DK] + bc3[GLA_DK:2 * GLA_DK] + bc3[2 * GLA_DK:3 * GLA_DK])
    row1 = lax.shift_right_logical(lax.broadcasted_iota(jnp.int32, (T, 1), 0), sh)

    S = s_scr[...] if chain else None
    o_inter = []
    for g in range(G):
        rows = slice(g * C, (g + 1) * C)
        if not chain:
            S = s0_ref[g, 0]
        o_inter.append(_dot(qd32[rows].astype(BF16), S.astype(BF16)))
        if C % 16 == 0:
            kv = _dot_tn(kd32[rows].astype(BF16), v[rows])
        else:
            kv = _dot_tn(jnp.where(row1 == g, kd32, 0.0).astype(BF16), v)
        S = jnp.broadcast_to(dec_all[:, g:g + 1], S.shape) * S + kv
        if not chain:
            sout_ref[g, 0] = S
    o = o_intra + jnp.concatenate(o_inter, axis=0)
    rg = r_ref[...]
    o_ref[...] = (_rms(o, gg_ref[...]) * (rg * _sigmoid(rg))).astype(o_ref.dtype)

    if chain:
        s_scr[...] = S

        @pl.when(t == pl.num_programs(2) - 1)
        def _():
            sout_ref[0, 0] = S


def gla_branch(z, wa2p, ba, gg, s0, *, layer, batch, seq, chunk, tt, nb):
    n = z.shape[0]
    if seq < chunk:
        nb = min(nb, batch)
        seg, n_seg, chain, nt, rows = seq, nb, False, 1, nb * seq
    else:
        seg, n_seg, chain, nt, rows = chunk, tt // chunk, True, seq // tt, tt
        assert nb == 1 and batch == 1
    kern = functools.partial(_gla_kernel, seg=seg, n_seg=n_seg, chain=chain)
    qo, ko, vo, ro, ao = Z_QA // 128, Z_KA // 128, Z_VA // 256, Z_RA // 256, Z_ALR // 128
    rowmap = lambda b, h, t: b * nt + t
    return pl.pallas_call(
        kern,
        out_shape=(jax.ShapeDtypeStruct((n, GLA_HEADS * GLA_DV), BF16),
                   jax.ShapeDtypeStruct((batch, GLA_HEADS, GLA_DK, GLA_DV), F32)),
        grid=(batch // nb, GLA_HEADS, nt),
        in_specs=[pl.BlockSpec((rows, 128), lambda b, h, t: (rowmap(b, h, t), qo + h)),
                  pl.BlockSpec((rows, 128), lambda b, h, t: (rowmap(b, h, t), ko + h)),
                  pl.BlockSpec((rows, 256), lambda b, h, t: (rowmap(b, h, t), vo + h)),
                  pl.BlockSpec((rows, 256), lambda b, h, t: (rowmap(b, h, t), ro + h)),
                  pl.BlockSpec((rows, 128), lambda b, h, t: (rowmap(b, h, t), ao)),
                  pl.BlockSpec((128, 128), lambda b, h, t: (0, h)),
                  pl.BlockSpec((1, 128), lambda b, h, t: (0, h)),
                  pl.BlockSpec((1, 256), lambda b, h, t: (0, h)),
                  pl.BlockSpec((None, nb, 1, GLA_DK, GLA_DV), lambda b, h, t: (layer, b, h, 0, 0))],
        out_specs=(pl.BlockSpec((rows, 256), lambda b, h, t: (rowmap(b, h, t), h)),
                   pl.BlockSpec((nb, 1, GLA_DK, GLA_DV), lambda b, h, t: (b, h, 0, 0))),
        scratch_shapes=[pltpu.VMEM((GLA_DK, GLA_DV), F32)],
        compiler_params=_cp(("parallel", "parallel", "arbitrary")),
        name="gla",
    )(z, z, z, z, z, wa2p, ba, gg, s0)


def _ln_swish(y, lg, lb):
    mu = jnp.mean(y, axis=-1, keepdims=True)
    yc = y - mu
    var = jnp.mean(yc * yc, axis=-1, keepdims=True)
    yn = yc * lax.rsqrt(var + EPS) * lg + lb
    return yn * _sigmoid(yn)


def _conv_prompt_kernel(u_ref, g_ref, w_ref, cb_ref, lg_ref, lb_ref, o_ref, st_ref, win, ybuf, *, tt):
    i = pl.program_id(0)
    ch = u_ref.shape[1]

    @pl.when(i == 0)
    def _():
        win[0:32, :] = jnp.zeros((32, ch), F32)

    g = g_ref[...]
    win[32:32 + tt, :] = u_ref[...] * _sigmoid(g)
    for c0 in range(0, ch, LANE):
        cs = slice(c0, c0 + LANE)
        acc = jnp.zeros((tt, LANE), F32)
        for j in range(CONV_WIDTH):
            acc = acc + w_ref[j:j + 1, cs] * win[2 + j:2 + j + tt, cs]
        ybuf[:, cs] = acc + cb_ref[:, cs]
    o_ref[...] = _ln_swish(ybuf[...], lg_ref[...], lb_ref[...]).astype(o_ref.dtype)

    @pl.when(i == pl.num_programs(0) - 1)
    def _():
        st_ref[0] = win[tt + 2:tt + 32, :]

    win[0:32, :] = win[tt:tt + 32, :]


def conv_prompt(z, w, cb, lg, lb, *, tt=256):
    n = z.shape[0]
    ch = BRANCH_W
    uo, go = Z_GU // ch, Z_GG // ch
    return pl.pallas_call(
        functools.partial(_conv_prompt_kernel, tt=tt),
        out_shape=(jax.ShapeDtypeStruct((n, ch), BF16),
                   jax.ShapeDtypeStruct((1, HIST, ch), F32)),
        grid=(n // tt,),
        in_specs=[pl.BlockSpec((tt, ch), lambda i: (i, uo)),
                  pl.BlockSpec((tt, ch), lambda i: (i, go)),
                  pl.BlockSpec((CONV_WIDTH, ch), lambda i: (0, 0)),
                  pl.BlockSpec((1, ch), lambda i: (0, 0)),
                  pl.BlockSpec((1, ch), lambda i: (0, 0)),
                  pl.BlockSpec((1, ch), lambda i: (0, 0))],
        out_specs=(pl.BlockSpec((tt, ch), lambda i: (i, 0)),
                   pl.BlockSpec((1, HIST, ch), lambda i: (0, 0, 0))),
        scratch_shapes=[pltpu.VMEM((tt + 32, ch), F32), pltpu.VMEM((tt, ch), F32)],
        compiler_params=_cp(("arbitrary",)),
        name="conv_prompt",
    )(z, z, w, cb, lg, lb)


def _conv_sample_kernel(u_ref, g_ref, buf_ref, w_ref, cb_ref, lg_ref, lb_ref, o_ref, st_ref, win, *, nb, ts):
    ch = u_ref.shape[1]
    g = g_ref[...]
    a = u_ref[...] * _sigmoid(g)
    for i in range(nb):
        win[i, 0:32, :] = buf_ref[i]
        win[i, 32:32 + ts, :] = a[i * ts:(i + 1) * ts]
        acc = jnp.zeros((ts, ch), F32)
        for j in range(CONV_WIDTH):
            acc = acc + w_ref[j:j + 1, :] * win[i, 2 + j:2 + j + ts, :]
        y = acc + cb_ref[...]
        o_ref[i * ts:(i + 1) * ts, :] = _ln_swish(y, lg_ref[...], lb_ref[...]).astype(o_ref.dtype)
        st_ref[i] = win[i, 2 + ts:32 + ts, :]


def conv_sample(z, bufp, w, cb, lg, lb, *, batch, ts, nb=8):
    n = z.shape[0]
    ch = BRANCH_W
    uo, go = Z_GU // ch, Z_GG // ch
    return pl.pallas_call(
        functools.partial(_conv_sample_kernel, nb=nb, ts=ts),
        out_shape=(jax.ShapeDtypeStruct((n, ch), BF16),
                   jax.ShapeDtypeStruct((batch, HIST, ch), F32)),
        grid=(batch // nb,),
        in_specs=[pl.BlockSpec((nb * ts, ch), lambda i: (i, uo)),
                  pl.BlockSpec((nb * ts, ch), lambda i: (i, go)),
                  pl.BlockSpec((nb, 32, ch), lambda i: (i, 0, 0)),
                  pl.BlockSpec((CONV_WIDTH, ch), lambda i: (0, 0)),
                  pl.BlockSpec((1, ch), lambda i: (0, 0)),
                  pl.BlockSpec((1, ch), lambda i: (0, 0)),
                  pl.BlockSpec((1, ch), lambda i: (0, 0))],
        out_specs=(pl.BlockSpec((nb * ts, ch), lambda i: (i, 0)),
                   pl.BlockSpec((nb, HIST, ch), lambda i: (i, 0, 0))),
        scratch_shapes=[pltpu.VMEM((nb, 32 + ts, ch), F32)],
        compiler_params=_cp(("parallel",)),
        name="conv_sample",
    )(z, z, bufp, w, cb, lg, lb)


def _rope_pair(v, gpair, tab):
    lane = lax.broadcasted_iota(jnp.int32, v.shape, 1)
    ms = jnp.sum(jnp.where(lane < MLA_ROPE, v * v, 0.0), axis=-1, keepdims=True) * (1.0 / MLA_ROPE)
    t = v * lax.rsqrt(ms + EPS) * gpair * tab
    return t + pltpu.roll(t, MLA_ROPE, axis=1)


def _mla_prep_kernel(cq_ref, ckv_ref, krp_ref, tab_ref, gcq_ref, gckv_ref, gkpe_ref, wuq_ref, gqn_ref, gqr_ref,
                     wk_ref, gkn_ref, *out_refs, sample):
    if sample:
        ckv_o, kpe_o, qa_o, qp_o = out_refs
    else:
        ckv_o, kpe_o, qa_o, kc_o, ckvb_o = out_refs
    tab = tab_ref[...]
    lane = lax.broadcasted_iota(jnp.int32, tab.shape, 1)
    ckv = _rms(ckv_ref[...], gckv_ref[...])
    ckv_o[...] = ckv
    kpe = _rope_pair(krp_ref[...], gkpe_ref[...], tab)
    kpe_o[...] = kpe[:, :MLA_ROPE]
    kpe_z = jnp.where(lane < MLA_ROPE, kpe, 0.0)
    cqn = _rms(cq_ref[...], gcq_ref[...]).astype(BF16)
    ckv_b = ckv.astype(BF16)
    if not sample:
        ckvb_o[...] = ckv.T.astype(BF16)
    for h in range(MLA_HEADS):
        qh = _dot(cqn, wuq_ref[h])
        qn = _rms(qh[:, :MLA_NOPE], gqn_ref[...])
        qp = _rope_pair(qh[:, MLA_NOPE:], gqr_ref[...], tab)
        qp_z = jnp.where(lane < MLA_ROPE, qp, 0.0) * MLA_SCALE
        if sample:
            qg = (qn * gkn_ref[...] * MLA_SCALE).astype(BF16)
            qa_o[:, h * 256:(h + 1) * 256] = _dot(qg, wk_ref[h]).astype(BF16)
            qp_o[:, h * LANE:(h + 1) * LANE] = qp_z.astype(BF16)
        else:
            qa_o[h, 0:LANE, :] = (qn * MLA_SCALE).T.astype(BF16)
            qa_o[h, LANE:2 * LANE, :] = qp_z.T.astype(BF16)
            kn = _rms(_dot(ckv_b, wk_ref[h]), gkn_ref[...])
            kc_o[h, :, 0:LANE] = kn.astype(BF16)
            kc_o[h, :, LANE:2 * LANE] = kpe_z.astype(BF16)


def mla_prep(z, tab, gcq, gckv, gkpe2, wuq, gqn, gqr2, wk, gkn, *, sample, tm=256):
    n = z.shape[0]
    tm = min(tm, n)
    H = MLA_HEADS
    full = lambda *s: pl.BlockSpec(s, lambda i: (0,) * len(s))
    in_specs = [pl.BlockSpec((tm, 512), lambda i: (i, Z_CQ // 512)),
                pl.BlockSpec((tm, 256), lambda i: (i, Z_CKV // 256)),
                pl.BlockSpec((tm, 128), lambda i: (i, Z_KR // 128)),
                pl.BlockSpec((tm, 128), lambda i: (i, 0)),
                full(1, 512), full(1, 256), full(1, 128),
                full(H, 512, 256), full(1, 128), full(1, 128),
                full(*wk.shape), full(1, 128)]
    if sample:
        out_shape = (jax.ShapeDtypeStruct((n, 256), F32), jax.ShapeDtypeStruct((n, MLA_ROPE), F32),
                     jax.ShapeDtypeStruct((n, H * 256), BF16), jax.ShapeDtypeStruct((n, H * LANE), BF16))
        out_specs = (pl.BlockSpec((tm, 256), lambda i: (i, 0)), pl.BlockSpec((tm, MLA_ROPE), lambda i: (i, 0)),
                     pl.BlockSpec((tm, H * 256), lambda i: (i, 0)), pl.BlockSpec((tm, H * LANE), lambda i: (i, 0)))
    else:
        out_shape = (jax.ShapeDtypeStruct((n, 256), F32), jax.ShapeDtypeStruct((n, MLA_ROPE), F32),
                     jax.ShapeDtypeStruct((H, 256, n), BF16), jax.ShapeDtypeStruct((H, n, 256), BF16),
                     jax.ShapeDtypeStruct((256, n), BF16))
        out_specs = (pl.BlockSpec((tm, 256), lambda i: (i, 0)), pl.BlockSpec((tm, MLA_ROPE), lambda i: (i, 0)),
                     pl.BlockSpec((H, 256, tm), lambda i: (0, 0, i)), pl.BlockSpec((H, tm, 256), lambda i: (0, i, 0)),
                     pl.BlockSpec((256, tm), lambda i: (0, i)))
    return pl.pallas_call(
        functools.partial(_mla_prep_kernel, sample=sample),
        out_shape=out_shape, grid=(n // tm,), in_specs=in_specs, out_specs=out_specs,
        compiler_params=_cp(("parallel",)),
        name="mla_prep_sample" if sample else "mla_prep_prompt",
    )(z, z, z, tab, gcq, gckv, gkpe2, wuq, gqn, gqr2, wk, gkn)


def _flash_kernel(qT_ref, k_ref, vT_ref, wuvT_ref, o_ref, m_scr, l_scr, acc_scr, *, tq):
    i = pl.program_id(0)
    j = pl.program_id(1)
    H = qT_ref.shape[0]

    @pl.when(j == 0)
    def _():
        m_scr[...] = jnp.full(m_scr.shape, -jnp.inf, F32)
        l_scr[...] = jnp.zeros(l_scr.shape, F32)
        acc_scr[...] = jnp.zeros(acc_scr.shape, F32)

    def update(masked):
        vT = vT_ref[...]
        krow = lax.broadcasted_iota(jnp.int32, (tq, tq), 0)
        qcol = lax.broadcasted_iota(jnp.int32, (tq, tq), 1)
        for h in range(H):
            sT = _dot(k_ref[h], qT_ref[h])
            if masked:
                sT = jnp.where(krow <= qcol, sT, -jnp.inf)
            m_prev = m_scr[h]
            m_new = jnp.maximum(m_prev, jnp.max(sT, axis=0, keepdims=True))
            alpha = jnp.exp(m_prev - m_new)
            pT = jnp.exp(sT - m_new)
            l_scr[h] = alpha * l_scr[h] + jnp.sum(pT, axis=0, keepdims=True)
            acc_scr[h] = alpha * acc_scr[h] + _dot(vT, pT.astype(BF16))
            m_scr[h] = m_new

    @pl.when(j < i)
    def _():
        update(False)

    @pl.when(j == i)
    def _():
        update(True)
        for h in range(H):
            o_latT = (acc_scr[h] / l_scr[h]).astype(BF16)
            oT = _dot(wuvT_ref[h], o_latT)
            o_ref[:, h * MLA_V:(h + 1) * MLA_V] = oT.T.astype(o_ref.dtype)


def flash_prompt(qcT, kc, ckvT_b, wuvT, *, tq=1024):
    H, dk, n = qcT.shape
    tq = min(tq, n)
    nq = n // tq
    return pl.pallas_call(
        functools.partial(_flash_kernel, tq=tq),
        out_shape=jax.ShapeDtypeStruct((n, H * MLA_V), BF16),
        grid=(nq, nq),
        in_specs=[pl.BlockSpec((H, dk, tq), lambda i, j: (0, 0, i)),
                  pl.BlockSpec((H, tq, dk), lambda i, j: (0, jnp.minimum(i, j), 0)),
                  pl.BlockSpec((256, tq), lambda i, j: (0, jnp.minimum(i, j))),
                  pl.BlockSpec((H, MLA_V, 256), lambda i, j: (0, 0, 0))],
        out_specs=pl.BlockSpec((tq, H * MLA_V), lambda i, j: (i, 0)),
        scratch_shapes=[pltpu.VMEM((H, 1, tq), F32), pltpu.VMEM((H, 1, tq), F32), pltpu.VMEM((H, 256, tq), F32)],
        compiler_params=_cp(("parallel", "arbitrary")),
        name="flash_prompt",
    )(qcT, kc, ckvT_b, wuvT)


ATTEND_PAGES_PER_STEP = 64
ATTEND_SUB_KEYS = 1024


def _attend_sample_kernel(pt_ref, ckv_hbm, kpeT_hbm, qa_ref, qp_ref, cnew_ref, pnewT_ref, wkt_ref, wuv_ref, o_ref,
                          cbuf, kbuf, lhs, m_scr, l_scr, acc_scr, sem, *, layer, ts, npg, n, page):
    b = pl.program_id(0)
    j = pl.program_id(1)
    nb = pl.num_programs(0)
    nj = pl.num_programs(1)
    step = b * nj + j
    slot = step % 2
    H = MLA_HEADS
    R = H * ts
    NK = H * MLA_NOPE
    tk = npg * page
    ppn = n // page

    def page_copies(bb, jj, sl):
        out = []
        for p in range(npg):
            pg = pt_ref[bb, jj * npg + p]
            out.append(pltpu.make_async_copy(ckv_hbm.at[layer, pg], cbuf.at[sl, p], sem.at[sl, 0]))
            out.append(pltpu.make_async_copy(kpeT_hbm.at[layer, pg], kbuf.at[sl, p], sem.at[sl, 1]))
        return out

    @pl.when(step == 0)
    def _():
        lhs[0:NK, :] = wkt_ref[...]
        for c in page_copies(0, 0, 0):
            c.start()

    @pl.when(step + 1 < nb * nj)
    def _():
        last_j = j + 1 == nj
        for c in page_copies(jnp.where(last_j, b + 1, b), jnp.where(last_j, 0, j + 1), 1 - slot):
            c.start()

    pltpu.make_async_copy(ckv_hbm.at[layer, pl.ds(0, npg)], cbuf.at[slot], sem.at[slot, 0]).wait()
    pltpu.make_async_copy(kpeT_hbm.at[layer, pl.ds(0, npg)], kbuf.at[slot], sem.at[slot, 1]).wait()

    @pl.when(j == 0)
    def _():
        lhs[NK:NK + R, :] = qa_ref[0]
        m_scr[...] = jnp.full(m_scr.shape, -jnp.inf, F32)
        l_scr[...] = jnp.zeros(l_scr.shape, F32)
        acc_scr[...] = jnp.zeros(acc_scr.shape, F32)

    qp = qp_ref[0][:, :MLA_ROPE]

    def scores(cb, kpT_b):
        m = cb.shape[0]
        big = _dot_nt(lhs[...], cb)
        kn = big[0:NK].reshape(H, MLA_NOPE, m)
        rinv = lax.rsqrt(jnp.sum(kn * kn, axis=1) * (1.0 / MLA_NOPE) + EPS)
        s = big[NK:NK + R].reshape(H, ts, m) * rinv[:, None, :]
        return s.reshape(R, m) + _dot(qp, kpT_b)

    def update(s, cb):
        m_prev = m_scr[...]
        m_new = jnp.maximum(m_prev, jnp.max(s, axis=-1, keepdims=True))
        alpha = jnp.exp(m_prev - m_new)
        p = jnp.exp(s - m_new)
        l_scr[...] = alpha * l_scr[...] + jnp.sum(p, axis=-1, keepdims=True)
        acc_scr[...] = alpha * acc_scr[...] + _dot(p.astype(BF16), cb)
        m_scr[...] = m_new

    prev = None
    for i in range(tk // n):
        cb = cbuf[slot, i * ppn:(i + 1) * ppn].reshape(n, 256).astype(BF16)
        kpb = jnp.concatenate([kbuf[slot, i * ppn + t] for t in range(ppn)], axis=1).astype(BF16)
        s = scores(cb, kpb)
        if prev is not None:
            update(*prev)
        prev = (s, cb)
    update(*prev)

    @pl.when(j == nj - 1)
    def _():
        c = jnp.concatenate([cnew_ref[...], jnp.zeros((LANE - ts, 256), F32)], axis=0).astype(BF16)
        key = lax.broadcasted_iota(jnp.int32, (R, LANE), 1)
        qi = lax.broadcasted_iota(jnp.int32, (R, LANE), 0) % ts
        s = jnp.where(key <= qi, scores(c, pnewT_ref[0].astype(BF16)), -jnp.inf)
        update(s, c)
        o_lat = acc_scr[...] / l_scr[...]
        for h in range(H):
            o_ref[:, h * MLA_V:(h + 1) * MLA_V] = _dot(o_lat[h * ts:(h + 1) * ts].astype(BF16), wuv_ref[h])


def attend_sample(page_table, cache_ckv, cache_kpeT, layer, qa, qp, ckv_new, kpe_newT, wkt, wuv, *, batch, ts):
    page = cache_ckv.shape[2]
    n_pages = page_table.shape[1]
    npg = math.gcd(ATTEND_PAGES_PER_STEP, n_pages)
    n = min(ATTEND_SUB_KEYS, npg * page)
    H = MLA_HEADS
    R = H * ts
    in_specs = [pl.BlockSpec(memory_space=pl.ANY), pl.BlockSpec(memory_space=pl.ANY),
                pl.BlockSpec((1, R, 256), lambda b, j, pt: (b, 0, 0)),
                pl.BlockSpec((1, R, LANE), lambda b, j, pt: (b, 0, 0)),
                pl.BlockSpec((ts, 256), lambda b, j, pt: (b, 0)),
                pl.BlockSpec((1, MLA_ROPE, LANE), lambda b, j, pt: (b, 0, 0)),
                pl.BlockSpec((H * MLA_NOPE, 256), lambda b, j, pt: (0, 0)),
                pl.BlockSpec((H, 256, MLA_V), lambda b, j, pt: (0, 0, 0))]
    grid_spec = pltpu.PrefetchScalarGridSpec(
        num_scalar_prefetch=1, grid=(batch, n_pages // npg), in_specs=in_specs,
        out_specs=pl.BlockSpec((ts, H * MLA_V), lambda b, j, pt: (b, 0)),
        scratch_shapes=[pltpu.VMEM((2, npg, page, 256), F32), pltpu.VMEM((2, npg, MLA_ROPE, page), F32),
                        pltpu.VMEM((H * MLA_NOPE + R, 256), BF16), pltpu.VMEM((R, 1), F32),
                        pltpu.VMEM((R, 1), F32), pltpu.VMEM((R, 256), F32), pltpu.SemaphoreType.DMA((2, 2))])
    return pl.pallas_call(
        functools.partial(_attend_sample_kernel, layer=layer, ts=ts, npg=npg, n=n, page=page),
        out_shape=jax.ShapeDtypeStruct((batch * ts, H * MLA_V), F32),
        grid_spec=grid_spec,
        compiler_params=_cp(("arbitrary", "arbitrary")),
        name="attend_sample",
    )(page_table, cache_ckv, cache_kpeT, qa, qp, ckv_new, kpe_newT, wkt, wuv)


def _merge_kernel(oa_ref, ob_ref, oc_ref, ga_ref, gb_ref, gc_ref, wa_ref, wb_ref, wc_ref, m_ref):
    acc = _sigmoid(ga_ref[...]) * _dot(oa_ref[...].astype(BF16), wa_ref[0])
    acc = acc + _sigmoid(gb_ref[...]) * _dot(ob_ref[...].astype(BF16), wb_ref[0])
    acc = acc + _sigmoid(gc_ref[...]) * _dot(oc_ref[...].astype(BF16), wc_ref[0])
    m_ref[...] = acc.astype(m_ref.dtype)


def branch_merge(oa, ob, oc, z, wbr, *, tm=512, tn=512):
    n = oa.shape[0]
    tm = min(tm, n)
    d = wbr.shape[2]
    w = oa.shape[1]
    nj = d // tn
    ospec = pl.BlockSpec((tm, w), lambda i, j: (i, 0))
    gspec = lambda k: pl.BlockSpec((tm, tn), lambda i, j: (i, k * nj + j))
    wspec = lambda k: pl.BlockSpec((1, w, tn), lambda i, j: (k, 0, j))
    return pl.pallas_call(
        _merge_kernel,
        out_shape=jax.ShapeDtypeStruct((n, d), BF16),
        grid=(n // tm, nj),
        in_specs=[ospec, ospec, ospec, gspec(0), gspec(1), gspec(2), wspec(0), wspec(1), wspec(2)],
        out_specs=pl.BlockSpec((tm, tn), lambda i, j: (i, j)),
        compiler_params=_cp(("parallel", "arbitrary")),
        name="branch_merge",
    )(oa, ob, oc, z, z, z, wbr, wbr, wbr)


def _out_proj_kernel(m_ref, w_ref, x_ref, o_ref):
    o_ref[...] = x_ref[...] + _dot(m_ref[...], w_ref[...])


def out_proj(m, w, x, *, tm=512, tn=512):
    n, d = x.shape
    tm = min(tm, n)
    return pl.pallas_call(
        _out_proj_kernel,
        out_shape=jax.ShapeDtypeStruct((n, d), F32),
        grid=(n // tm, d // tn),
        in_specs=[pl.BlockSpec((tm, d), lambda i, j: (i, 0)),
                  pl.BlockSpec((d, tn), lambda i, j: (0, j)),
                  pl.BlockSpec((tm, tn), lambda i, j: (i, j))],
        out_specs=pl.BlockSpec((tm, tn), lambda i, j: (i, j)),
        compiler_params=_cp(("parallel", "arbitrary")),
        name="out_proj",
    )(m, w, x)


def _swiglu_acc(h, wg, wu, wd):
    a = _dot(h, wg.astype(BF16))
    u = _dot(h, wu.astype(BF16))
    return _dot((a * _sigmoid(a) * u).astype(BF16), wd.astype(BF16))


def _ffn_kernel(x_ref, g_ref, wg_ref, wu_ref, wd_ref, o_ref, h_ref):
    @pl.when(pl.program_id(1) == 0)
    def _():
        x = x_ref[...]
        h_ref[...] = _rms(x, g_ref[...]).astype(BF16)
        o_ref[...] = x
    o_ref[...] += _swiglu_acc(h_ref[...], wg_ref[...], wu_ref[...], wd_ref[...])


def ffn_dense(x, g, wg, wu, wd, *, tm=512, tf=512):
    n, d = x.shape
    dff = wg.shape[1]
    tm = min(tm, n)
    return pl.pallas_call(
        _ffn_kernel,
        out_shape=jax.ShapeDtypeStruct((n, d), F32),
        grid=(n // tm, dff // tf),
        in_specs=[pl.BlockSpec((tm, d), lambda i, f: (i, 0)),
                  pl.BlockSpec((1, d), lambda i, f: (0, 0)),
                  pl.BlockSpec((d, tf), lambda i, f: (0, f)),
                  pl.BlockSpec((d, tf), lambda i, f: (0, f)),
                  pl.BlockSpec((tf, d), lambda i, f: (f, 0))],
        out_specs=pl.BlockSpec((tm, d), lambda i, f: (i, 0)),
        scratch_shapes=[pltpu.VMEM((tm, d), BF16)],
        compiler_params=_cp(("parallel", "arbitrary")),
        name="ffn_dense",
    )(x, g, wg, wu, wd)


MOE_TILE = 1024


def _route_kernel(x_ref, g_ref, wr_ref, h_ref, ti_ref, tw_ref):
    h = _rms(x_ref[...], g_ref[...])
    h_ref[...] = h.astype(BF16)
    logits = _dot(h, wr_ref[...], precision=HI)
    lane = lax.broadcasted_iota(jnp.int32, logits.shape, 1)
    logits = jnp.where(lane < N_EXPERTS, logits, -jnp.inf)
    m1 = jnp.max(logits, axis=-1, keepdims=True)
    i1 = jnp.min(jnp.where(logits == m1, lane, LANE), axis=-1, keepdims=True)
    rest = jnp.where(lane == i1, -jnp.inf, logits)
    m2 = jnp.max(rest, axis=-1, keepdims=True)
    i2 = jnp.min(jnp.where(rest == m2, lane, LANE), axis=-1, keepdims=True)
    e2 = jnp.exp(m2 - m1)
    w1 = 1.0 / (1.0 + e2)
    w2 = e2 / (1.0 + e2)
    ti_ref[...] = jnp.where(lane == 0, i1, jnp.where(lane == 1, i2, 0))
    tw_ref[...] = jnp.where(lane == 0, w1, jnp.where(lane == 1, w2, 0.0))


def moe_route(x, g, wr, *, tm=512):
    n, d = x.shape
    tm = min(tm, n)
    row = lambda w: pl.BlockSpec((tm, w), lambda i: (i, 0))
    return pl.pallas_call(
        _route_kernel,
        out_shape=(jax.ShapeDtypeStruct((n, d), BF16), jax.ShapeDtypeStruct((n, LANE), jnp.int32),
                   jax.ShapeDtypeStruct((n, LANE), F32)),
        grid=(n // tm,),
        in_specs=[row(d), pl.BlockSpec((1, d), lambda i: (0, 0)), pl.BlockSpec((d, LANE), lambda i: (0, 0))],
        out_specs=(row(d), row(LANE), row(LANE)),
        compiler_params=_cp(("parallel",)),
        name="moe_route",
    )(x, g, wr)


def _gather_rows_kernel(idx_ref, src_hbm, o_ref, sem, *, tile):
    def issue(r, carry):
        pltpu.make_async_copy(src_hbm.at[pl.ds(idx_ref[0, 0, r], 1)], o_ref.at[pl.ds(r, 1)], sem).start()
        return carry
    lax.fori_loop(0, tile, issue, 0, unroll=8)
    pltpu.make_async_copy(src_hbm.at[pl.ds(0, tile)], o_ref, sem).wait()


def gather_rows(idx, src, *, tile):
    n_out = idx.shape[0]
    return pl.pallas_call(
        functools.partial(_gather_rows_kernel, tile=tile),
        out_shape=jax.ShapeDtypeStruct((n_out,) + src.shape[1:], src.dtype),
        grid=(n_out // tile,),
        in_specs=[pl.BlockSpec((1, 1, tile), lambda t: (t, 0, 0), memory_space=pltpu.SMEM),
                  pl.BlockSpec(memory_space=pl.ANY)],
        out_specs=pl.BlockSpec((tile,) + src.shape[1:], lambda t: (t, 0, 0)),
        scratch_shapes=[pltpu.SemaphoreType.DMA(())],
        compiler_params=_cp(("arbitrary",)),
        name="moe_gather",
    )(idx.reshape(n_out // tile, 1, tile), src)


def _moe_ffn_kernel(te_ref, nv_ref, x_ref, wg_ref, wu_ref, wd_ref, o_ref, acc):
    t = pl.program_id(0)
    f = pl.program_id(1)

    @pl.when(t < nv_ref[0])
    def _():
        y = _swiglu_acc(x_ref[...], wg_ref[0], wu_ref[0], wd_ref[0])

        @pl.when(f == 0)
        def _():
            acc[...] = y

        @pl.when(f > 0)
        def _():
            acc[...] += y

        @pl.when(f == pl.num_programs(1) - 1)
        def _():
            o_ref[...] = acc[...].astype(o_ref.dtype)

    @pl.when(jnp.logical_and(t >= nv_ref[0], f == pl.num_programs(1) - 1))
    def _():
        o_ref[...] = jnp.zeros(o_ref.shape, o_ref.dtype)


def moe_ffn(tile_expert, n_valid_tiles, xs, wg, wu, wd, *, tile, tf=512):
    ns, d = xs.shape
    dff = wg.shape[2]
    grid_spec = pltpu.PrefetchScalarGridSpec(
        num_scalar_prefetch=2, grid=(ns // tile, dff // tf),
        in_specs=[pl.BlockSpec((tile, d), lambda t, f, te, nv: (t, 0)),
                  pl.BlockSpec((1, d, tf), lambda t, f, te, nv: (te[t], 0, jnp.where(t < nv[0], f, dff // tf - 1))),
                  pl.BlockSpec((1, d, tf), lambda t, f, te, nv: (te[t], 0, jnp.where(t < nv[0], f, dff // tf - 1))),
                  pl.BlockSpec((1, tf, d), lambda t, f, te, nv: (te[t], jnp.where(t < nv[0], f, dff // tf - 1), 0))],
        out_specs=pl.BlockSpec((tile, d), lambda t, f, te, nv: (t, 0)),
        scratch_shapes=[pltpu.VMEM((tile, d), F32)])
    return pl.pallas_call(
        _moe_ffn_kernel,
        out_shape=jax.ShapeDtypeStruct((ns, d), BF16),
        grid_spec=grid_spec,
        compiler_params=_cp(("arbitrary", "arbitrary")),
        name="moe_ffn",
    )(tile_expert, n_valid_tiles, xs, wg, wu, wd)


def _moe_combine_kernel(s1_ref, s2_ref, x_ref, w_ref, y_hbm, o_ref, y1, y2, sem, *, tm):
    def issue(r, carry):
        pltpu.make_async_copy(y_hbm.at[pl.ds(s1_ref[0, 0, r], 1)], y1.at[pl.ds(r, 1)], sem.at[0]).start()
        pltpu.make_async_copy(y_hbm.at[pl.ds(s2_ref[0, 0, r], 1)], y2.at[pl.ds(r, 1)], sem.at[1]).start()
        return carry
    lax.fori_loop(0, tm, issue, 0)
    pltpu.make_async_copy(y_hbm.at[pl.ds(0, tm)], y1, sem.at[0]).wait()
    pltpu.make_async_copy(y_hbm.at[pl.ds(0, tm)], y2, sem.at[1]).wait()
    w = w_ref[...]
    o_ref[...] = x_ref[...] + w[0] * y1[...].astype(F32) + w[1] * y2[...].astype(F32)


def moe_combine(slot1, slot2, x3, w3, y3, *, tm=256):
    n = x3.shape[0]
    tm = min(tm, n)
    sspec = pl.BlockSpec((1, 1, tm), lambda i: (i, 0, 0), memory_space=pltpu.SMEM)
    slab = pl.BlockSpec((tm,) + x3.shape[1:], lambda i: (i, 0, 0))
    return pl.pallas_call(
        functools.partial(_moe_combine_kernel, tm=tm),
        out_shape=jax.ShapeDtypeStruct(x3.shape, F32),
        grid=(n // tm,),
        in_specs=[sspec, sspec, slab,
                  pl.BlockSpec((2, tm, 1, LANE), lambda i: (0, i, 0, 0)),
                  pl.BlockSpec(memory_space=pl.ANY)],
        out_specs=slab,
        scratch_shapes=[pltpu.VMEM((tm,) + y3.shape[1:], y3.dtype), pltpu.VMEM((tm,) + y3.shape[1:], y3.dtype),
                        pltpu.SemaphoreType.DMA((2,))],
        compiler_params=_cp(("arbitrary",)),
        name="moe_combine",
    )(slot1.reshape(n // tm, 1, tm), slot2.reshape(n // tm, 1, tm), x3, w3, y3)


def moe_mixer(xs_list, g, wr, wg, wu, wd):
    routed = [moe_route(x, g, wr) for x in xs_list]
    h = jnp.concatenate([r[0] for r in routed], axis=0)
    ti = jnp.concatenate([r[1][:, :2] for r in routed], axis=0)
    n = h.shape[0]
    tile = MOE_TILE
    e_flat = ti.reshape(-1)
    onehot = (e_flat[:, None] == jnp.arange(N_EXPERTS, dtype=jnp.int32)[None, :]).astype(jnp.int32)
    csum = jnp.cumsum(onehot, axis=0)
    rank = jnp.sum(csum * onehot, axis=1) - 1
    counts = csum[-1]
    gpad = ((counts + tile - 1) // tile) * tile
    gend = jnp.cumsum(gpad)
    slot = (gend - gpad)[e_flat] + rank
    n_slots = ((2 * n + N_EXPERTS * (tile - 1)) // tile) * tile
    n_tiles = n_slots // tile
    nv = (gend[-1] // tile).astype(jnp.int32).reshape(1)
    tok_of_slot = jnp.zeros((n_slots,), jnp.int32).at[slot].set(jnp.arange(2 * n, dtype=jnp.int32) // 2)
    tstart = jnp.arange(n_tiles, dtype=jnp.int32) * tile
    te = jnp.sum((tstart[:, None] >= gend[None, :]).astype(jnp.int32), axis=1)
    te = jnp.minimum(te, te[jnp.maximum(nv[0] - 1, 0)]).astype(jnp.int32)

    d = h.shape[1]
    slab = (d // LANE, LANE)
    xs_sorted = gather_rows(tok_of_slot, h.reshape((n,) + slab), tile=tile).reshape(n_slots, d)
    y3 = moe_ffn(te, nv, xs_sorted, wg, wu, wd, tile=tile).reshape((n_slots,) + slab)
    slot2 = slot.reshape(n, 2)
    outs, off = [], 0
    for x, r in zip(xs_list, routed):
        m = x.shape[0]
        w3 = jnp.broadcast_to(r[2][:, :2].T[:, :, None, None], (2, m, 1, LANE))
        o3 = moe_combine(slot2[off:off + m, 0], slot2[off:off + m, 1], x.reshape((m,) + slab), w3, y3)
        outs.append(o3.reshape(m, d))
        off += m
    return outs


def _rope_table(pos):
    half = MLA_ROPE // 2
    inv = ROPE_THETA ** (-jnp.arange(half, dtype=F32) / half)
    ang = pos.astype(F32)[:, None] * inv[None, :]
    c, s = jnp.cos(ang), jnp.sin(ang)
    return jnp.concatenate([c, c, -s, s], axis=-1)


def _swap_pair(g):
    half = MLA_ROPE // 2
    return jnp.concatenate([g, g[half:], g[:half]])[None, :]


def kernel(x_prompt, x_sample, cache_kv_latent, cache_k_rope, state_gla, state_conv, page_table, g_mix, w_in, b_in, gla_w_a2, gla_b_a, gla_g_out, conv_w, conv_b, conv_ln_g, conv_ln_b, mla_g_cq, mla_g_ckv, mla_g_kpe, mla_w_uq, mla_g_qn, mla_g_qr, mla_w_uk, mla_g_kn, mla_w_uv, w_branch, w_out, g_ffn, ffn_w_gate, ffn_w_up, ffn_w_down, moe_w_router, moe_w_gate, moe_w_up, moe_w_down):
    depth = w_in.shape[0]
    bp, tp, d = x_prompt.shape
    bs, ts, _ = x_sample.shape
    assert bp == 1
    past = page_table.shape[1] * cache_kv_latent.shape[2]
    H = MLA_HEADS
    half = MLA_ROPE // 2

    xp = x_prompt.reshape(bp * tp, d)
    xs = x_sample.reshape(bs * ts, d)
    tab_p = _rope_table(jnp.arange(tp, dtype=jnp.int32))
    tab_s = jnp.tile(_rope_table(past + jnp.arange(ts, dtype=jnp.int32)), (bs, 1))
    s0_p = jnp.zeros((1, bp, GLA_HEADS, GLA_DK, GLA_DV), F32)
    cache_kpeT = jnp.swapaxes(cache_k_rope, 2, 3)

    outs = [[] for _ in range(8)]
    for l in range(depth):
        w_in_r, b_in_r = _rearrange_in_proj(w_in[l], b_in[l])
        g_mix_l = g_mix[l][None, :]
        wa2p = jnp.pad(gla_w_a2[l], ((0, LANE - GLA_RANK), (0, 0)))
        ba = gla_b_a[l][None, :]
        gg = gla_g_out[l].reshape(1, GLA_HEADS * GLA_DV)
        cw, cb = conv_w[l], conv_b[l][None, :]
        lg, lb = conv_ln_g[l][None, :], conv_ln_b[l][None, :]
        gcq, gckv = mla_g_cq[l][None, :], mla_g_ckv[l][None, :]
        gkpe2, gqr2 = _swap_pair(mla_g_kpe[l]), _swap_pair(mla_g_qr[l])
        gqn, gkn = mla_g_qn[l][None, :], mla_g_kn[l][None, :]
        wq = mla_w_uq[l]
        wq_rope = wq[..., MLA_NOPE:]
        wuq = jnp.concatenate([wq, wq_rope[..., half:], wq_rope[..., :half]], axis=-1)
        wuq = wuq.transpose(1, 0, 2).astype(BF16)
        wk = mla_w_uk[l].transpose(1, 0, 2).astype(BF16)
        wkT = mla_w_uk[l].transpose(1, 2, 0).astype(BF16)
        wuv = mla_w_uv[l].transpose(1, 0, 2).astype(BF16)
        wuvT = mla_w_uv[l].transpose(1, 2, 0).astype(BF16)
        wbr = w_branch[l].astype(BF16)
        wo = w_out[l].astype(BF16)

        zp = in_proj(xp, g_mix_l, w_in_r, b_in_r)
        oa_p, sg_p = gla_branch(zp, wa2p, ba, gg, s0_p, layer=0, batch=bp, seq=tp, chunk=64, tt=512, nb=1)
        ob_p, sc_p = conv_prompt(zp, cw, cb, lg, lb)
        ckv_p, kpe_p, qcT_p, kc_p, ckvT_p = mla_prep(zp, tab_p, gcq, gckv, gkpe2, wuq, gqn, gqr2, wk, gkn, sample=False)
        oc_p = flash_prompt(qcT_p, kc_p, ckvT_p, wuvT)
        xp = out_proj(branch_merge(oa_p, ob_p, oc_p, zp, wbr), wo, xp)

        zs = in_proj(xs, g_mix_l, w_in_r, b_in_r)
        oa_s, sg_s = gla_branch(zs, wa2p, ba, gg, state_gla, layer=l, batch=bs, seq=ts, chunk=16, tt=ts, nb=16)
        bufp = jnp.pad(state_conv[l], ((0, 0), (2, 0), (0, 0)))
        ob_s, sc_s = conv_sample(zs, bufp, cw, cb, lg, lb, batch=bs, ts=ts)
        ckv_s, kpe_s, qa_s, qp_s = mla_prep(zs, tab_s, gcq, gckv, gkpe2, wuq, gqn, gqr2, wkT, gkn, sample=True)
        qa_s = qa_s.reshape(bs, ts, H, 256).transpose(0, 2, 1, 3).reshape(bs, H * ts, 256)
        qp_s = qp_s.reshape(bs, ts, H, LANE).transpose(0, 2, 1, 3).reshape(bs, H * ts, LANE)
        kpe_sT = jnp.pad(kpe_s.reshape(bs, ts, MLA_ROPE).transpose(0, 2, 1), ((0, 0), (0, 0), (0, LANE - ts)))
        oc_s = attend_sample(page_table, cache_kv_latent, cache_kpeT, l, qa_s, qp_s, ckv_s, kpe_sT,
                             wkT.reshape(H * MLA_NOPE, 256), wuv, batch=bs, ts=ts)
        xs = out_proj(branch_merge(oa_s, ob_s, oc_s, zs, wbr), wo, xs)

        gf = g_ffn[l][None, :]
        if l % 2 == 0:
            wg_, wu_, wd_ = (w[l // 2].astype(BF16) for w in (ffn_w_gate, ffn_w_up, ffn_w_down))
            xp = ffn_dense(xp, gf, wg_, wu_, wd_)
            xs = ffn_dense(xs, gf, wg_, wu_, wd_)
        else:
            wr = jnp.pad(moe_w_router[l // 2], ((0, 0), (0, LANE - N_EXPERTS)))
            wg_, wu_, wd_ = moe_w_gate[l // 2], moe_w_up[l // 2], moe_w_down[l // 2]
            xp, xs = moe_mixer([xp, xs], gf, wr, wg_, wu_, wd_)

        for k, v in enumerate((ckv_p.reshape(bp, tp, -1), kpe_p.reshape(bp, tp, -1), sg_p, sc_p,
                               ckv_s.reshape(bs, ts, -1), kpe_s.reshape(bs, ts, -1), sg_s, sc_s)):
            outs[k].append(v)

    return (xp.reshape(bp, tp, d), xs.reshape(bs, ts, d)) + tuple(jnp.stack(o) for o in outs)
```
